```python
import jax, jax.numpy as jnp
from jax import lax
import numpy as np

D_MODEL = 1024
BATCH = 2
SEQ = 16384
DEPTH = 2
DEC_BATCH = 8
DEC_SEQ = 32
PAST_LEN = 4096

CHUNK = 64
MIX_WIDTH = D_MODEL
HEAD_DIM = 64
A_WIDTH = MIX_WIDTH // 2
B_WIDTH = MIX_WIDTH - A_WIDTH
N_A_HEADS = A_WIDTH // HEAD_DIM
N_B_GROUPS = 8
B_GROUP_DIM = B_WIDTH // N_B_GROUPS
IN_WIDTH = 3 * A_WIDTH + 2 * B_WIDTH
MLP_CHUNK = 128
Q_BLOCK = 128
KEY_BLOCK = 128
SB_SCALE = HEAD_DIM ** -0.5
N_EXPERTS = 32
TOP_K = 4
D_FF = D_MODEL
SWIGLU_LIMIT = 7.0
SWIGLU_ALPHA = 1.702
MOE_BLOCK = 512
NORM_EPS = 1e-5

kernel_name = 'stickbreak_sgu_hymba_moe_stream_step'


def rms_norm(x, g):
    xf = x.astype(jnp.float32)
    y = xf * lax.rsqrt(jnp.mean(xf * xf, axis=-1, keepdims=True) + NORM_EPS)
    return (y * g.astype(jnp.float32)).astype(x.dtype)


def layer_norm(x, g, b):
    xf = x.astype(jnp.float32)
    xc = xf - jnp.mean(xf, axis=-1, keepdims=True)
    var = jnp.mean(xc * xc, axis=-1, keepdims=True)
    y = xc * lax.rsqrt(var + NORM_EPS) * g.astype(jnp.float32) + b.astype(jnp.float32)
    return y.astype(x.dtype)


def split_projection(hn, w_in):
    bsz, s, _ = hn.shape
    p = jnp.einsum('bsd,de->bse', hn, w_in)
    q, k, v, u, z = jnp.split(p, [A_WIDTH, 2 * A_WIDTH, 3 * A_WIDTH, 3 * A_WIDTH + B_WIDTH], axis=-1)
    to_heads = lambda t: t.reshape(bsz, s, N_A_HEADS, HEAD_DIM)
    return to_heads(q), to_heads(k), to_heads(v), u, z


def reverse_excl_cumsum(x):
    shp = x.shape
    nb = shp[-1] // KEY_BLOCK
    xb = x.reshape(shp[:-1] + (nb, KEY_BLOCK))
    idx = jnp.arange(KEY_BLOCK, dtype=jnp.int32)
    tri = (idx[:, None] > idx[None, :]).astype(x.dtype)
    within = jnp.einsum('...nj,js->...ns', xb, tri, precision=lax.Precision.HIGHEST)
    totals = jnp.sum(xb, axis=-1)
    bidx = jnp.arange(nb, dtype=jnp.int32)
    btri = (bidx[:, None] > bidx[None, :]).astype(x.dtype)
    later = jnp.einsum('...m,mn->...n', totals, btri, precision=lax.Precision.HIGHEST)
    return (within + later[..., None]).reshape(shp)


def stick_breaking_block(q, k, v, q_pos, k_pos):
    z = jnp.einsum('bqhd,bkhd->bhqk', q, k, preferred_element_type=jnp.float32) * SB_SCALE
    visible = k_pos[None, :] < q_pos[:, None]
    log_keep = jnp.where(visible, jax.nn.log_sigmoid(-z), 0.0)
    a = jnp.where(visible, jnp.exp(jax.nn.log_sigmoid(z) + reverse_excl_cumsum(log_keep)), 0.0)
    return jnp.einsum('bhqk,bkhd->bqhd', a.astype(v.dtype), v)


def stick_breaking_prompt(q, k, v):
    s = q.shape[1]
    pos = jnp.arange(s, dtype=jnp.int32)
    outs = []
    for i in range(s // Q_BLOCK):
        lo, hi = i * Q_BLOCK, (i + 1) * Q_BLOCK
        outs.append(stick_breaking_block(q[:, lo:hi], k[:, :hi], v[:, :hi], pos[lo:hi], pos[:hi]))
    return jnp.concatenate(outs, axis=1)


def stick_breaking_sample(q, k_new, v_new, past_k, past_v):
    n_past = past_k.shape[1]
    n_new = q.shape[1]
    n_keys = n_past + n_new
    n_pad = (-n_keys) % KEY_BLOCK
    pad = ((0, 0), (0, n_pad), (0, 0), (0, 0))
    k = jnp.pad(jnp.concatenate([past_k.astype(k_new.dtype), k_new], axis=1), pad)
    v = jnp.pad(jnp.concatenate([past_v.astype(v_new.dtype), v_new], axis=1), pad)
    k_pos = jnp.arange(n_keys + n_pad, dtype=jnp.int32)
    return stick_breaking_block(q, k, v, k_pos[n_past:n_keys], k_pos)


def sgu_inputs(u, z, ln_g, ln_b):
    bsz, s, _ = z.shape
    zg = jax.nn.gelu(z).reshape(bsz, s, N_B_GROUPS, B_GROUP_DIM)
    return jax.nn.gelu(u), layer_norm(zg, ln_g, ln_b)


def spatial_mask():
    blk = jnp.arange(MLP_CHUNK, dtype=jnp.int32) // CHUNK
    return blk[None, :] <= blk[:, None]


def sgu_mix_prompt(zn, w_spatial, b_spatial):
    bsz, s, g, c = zn.shape
    w = jnp.where(spatial_mask()[None], w_spatial, 0.0).astype(zn.dtype)
    zc = zn.reshape(bsz, s // MLP_CHUNK, MLP_CHUNK, g, c)
    mixed = jnp.einsum('gij,bnjgc->bnigc', w, zc) + b_spatial.T[None, None, :, :, None]
    return mixed.reshape(bsz, s, g * c)


def sgu_mix_sample(zn, w_spatial, b_spatial):
    bsz, n, g, c = zn.shape
    w = jnp.where(spatial_mask()[None], w_spatial, 0.0)[:, :n, :n].astype(zn.dtype)
    mixed = jnp.einsum('gij,bjgc->bigc', w, zn) + b_spatial[:, :n].T[None, :, :, None]
    return mixed.reshape(bsz, n, g * c)


def merge_heads(o_a, o_b, g_a, g_b, w_out):
    bsz, s = o_b.shape[:2]
    y = jnp.concatenate([rms_norm(o_a.reshape(bsz, s, A_WIDTH), g_a), rms_norm(o_b, g_b)], axis=-1)
    return jnp.einsum('bse,ed->bsd', y, w_out)


def moe_ffn(hn, w_router, b_router, w_gate_up, b_gate_up, w_down, b_down):
    shp = hn.shape
    xt = hn.reshape(-1, shp[-1])
    n_tok = xt.shape[0]
    logits = jnp.einsum('td,de->te', xt, w_router, preferred_element_type=jnp.float32) + b_router.astype(jnp.float32)
    top_val, top_exp = lax.top_k(logits, TOP_K)
    gates = jax.nn.softmax(top_val, axis=-1)
    n_rows = n_tok * TOP_K
    flat_exp = top_exp.reshape(-1)
    flat_tok = jnp.arange(n_rows, dtype=jnp.int32) // TOP_K
    flat_gate = gates.reshape(-1)
    order = jnp.argsort(flat_exp)
    sorted_exp = flat_exp[order]
    counts = jnp.zeros((N_EXPERTS,), jnp.int32).at[flat_exp].add(1)
    padded = (counts + MOE_BLOCK - 1) // MOE_BLOCK * MOE_BLOCK
    start = jnp.cumsum(counts) - counts
    pend = jnp.cumsum(padded)
    pstart = pend - padded
    slot = pstart[sorted_exp] + (jnp.arange(n_rows, dtype=jnp.int32) - start[sorted_exp])
    n_blocks = -(-n_rows // MOE_BLOCK) + N_EXPERTS
    n_slots = n_blocks * MOE_BLOCK
    slot_tok = jnp.full((n_slots,), n_tok, jnp.int32).at[slot].set(flat_tok[order])
    slot_gate = jnp.zeros((n_slots,), jnp.float32).at[slot].set(flat_gate[order])
    block_exp = jnp.minimum(jnp.searchsorted(pend, jnp.arange(n_blocks, dtype=jnp.int32) * MOE_BLOCK, side='right'), N_EXPERTS - 1)
    x_pad = jnp.concatenate([xt, jnp.zeros((1, xt.shape[1]), xt.dtype)], axis=0)

    def run_block(args):
        toks, e = args
        gu = x_pad[toks] @ w_gate_up[e] + b_gate_up[e]
        gate, lin = gu[:, :D_FF], gu[:, D_FF:]
        gate = jnp.minimum(gate, SWIGLU_LIMIT)
        lin = jnp.clip(lin, -SWIGLU_LIMIT, SWIGLU_LIMIT)
        act = gate * jax.nn.sigmoid(SWIGLU_ALPHA * gate) * (lin + 1.0)
        return act @ w_down[e] + b_down[e]

    out = lax.map(run_block, (slot_tok.reshape(n_blocks, MOE_BLOCK), block_exp))
    out = out.reshape(n_slots, -1) * slot_gate[:, None].astype(out.dtype)
    y = jnp.zeros_like(x_pad).at[slot_tok].add(out)[:n_tok]
    return y.reshape(shp)


def hybrid_layer(x, past_k, past_v, ln_mix_g, w_in, sgu_ln_g, sgu_ln_b, w_spatial, b_spatial,
                 out_norm_a_g, out_norm_b_g, w_out, ln_ffn_g, w_router, b_router,
                 w_gate_up, b_gate_up, w_down, b_down):
    hn = rms_norm(x, ln_mix_g)
    q, k, v, u, z = split_projection(hn, w_in)
    u, zn = sgu_inputs(u, z, sgu_ln_g, sgu_ln_b)
    if past_k is None:
        o_a = stick_breaking_prompt(q, k, v)
        o_b = u * sgu_mix_prompt(zn, w_spatial, b_spatial)
    else:
        o_a = stick_breaking_sample(q, k, v, past_k, past_v)
        o_b = u * sgu_mix_sample(zn, w_spatial, b_spatial)
    x = x + merge_heads(o_a, o_b, out_norm_a_g, out_norm_b_g, w_out)
    x = x + moe_ffn(rms_norm(x, ln_ffn_g), w_router, b_router, w_gate_up, b_gate_up, w_down, b_down)
    return x, k, v, zn


def setup_inputs(seed: int = 0) -> dict:
    key = jax.random.key(seed)
    ks = jax.random.split(key, 24)
    nrm = lambda k, shape, scale: jax.random.normal(k, shape, jnp.float32) * scale
    out_scale = (2 * DEPTH) ** -0.5
    return {
        'x_prompt': nrm(ks[0], (BATCH, SEQ, D_MODEL), 1.0),
        'x_sample': nrm(ks[1], (DEC_BATCH, DEC_SEQ, D_MODEL), 1.0),
        'cache_k': nrm(ks[2], (DEPTH, DEC_BATCH, PAST_LEN, N_A_HEADS, HEAD_DIM), 1.0),
        'cache_v': nrm(ks[3], (DEPTH, DEC_BATCH, PAST_LEN, N_A_HEADS, HEAD_DIM), 1.0),
        'ln_mix_g': 1.0 + nrm(ks[4], (DEPTH, D_MODEL), 0.02),
        'w_in': nrm(ks[5], (DEPTH, D_MODEL, IN_WIDTH), D_MODEL ** -0.5),
        'sgu_ln_g': 1.0 + nrm(ks[6], (DEPTH, N_B_GROUPS, B_GROUP_DIM), 0.02),
        'sgu_ln_b': nrm(ks[7], (DEPTH, N_B_GROUPS, B_GROUP_DIM), 0.02),
        'w_spatial': nrm(ks[8], (DEPTH, N_B_GROUPS, MLP_CHUNK, MLP_CHUNK), MLP_CHUNK ** -0.5),
        'b_spatial': 1.0 + nrm(ks[9], (DEPTH, N_B_GROUPS, MLP_CHUNK), 0.1),
        'out_norm_a_g': 1.0 + nrm(ks[10], (DEPTH, A_WIDTH), 0.02),
        'out_norm_b_g': 1.0 + nrm(ks[11], (DEPTH, B_WIDTH), 0.02),
        'w_out': nrm(ks[12], (DEPTH, MIX_WIDTH, D_MODEL), MIX_WIDTH ** -0.5 * out_scale),
        'ln_ffn_g': 1.0 + nrm(ks[13], (DEPTH, D_MODEL), 0.02),
        'w_router': nrm(ks[14], (DEPTH, D_MODEL, N_EXPERTS), D_MODEL ** -0.5),
        'b_router': nrm(ks[15], (DEPTH, N_EXPERTS), 0.01),
        'w_gate_up': nrm(ks[16], (DEPTH, N_EXPERTS, D_MODEL, 2 * D_FF), D_MODEL ** -0.5),
        'b_gate_up': nrm(ks[17], (DEPTH, N_EXPERTS, 2 * D_FF), 0.02),
        'w_down': nrm(ks[18], (DEPTH, N_EXPERTS, D_FF, D_MODEL), D_FF ** -0.5 * out_scale),
        'b_down': nrm(ks[19], (DEPTH, N_EXPERTS, D_MODEL), 0.02),
        'final_norm_g': 1.0 + nrm(ks[20], (D_MODEL,), 0.02),
    }


def reference(x_prompt, x_sample, cache_k, cache_v, ln_mix_g, w_in, sgu_ln_g, sgu_ln_b,
              w_spatial, b_spatial, out_norm_a_g, out_norm_b_g, w_out, ln_ffn_g,
              w_router, b_router, w_gate_up, b_gate_up, w_down, b_down, final_norm_g):
    xp, xs = x_prompt, x_sample
    kp, vp, ksm, vsm, zsm = [], [], [], [], []
    for l in range(DEPTH):
        weights = (ln_mix_g[l], w_in[l], sgu_ln_g[l], sgu_ln_b[l], w_spatial[l], b_spatial[l],
                   out_norm_a_g[l], out_norm_b_g[l], w_out[l], ln_ffn_g[l], w_router[l],
                   b_router[l], w_gate_up[l], b_gate_up[l], w_down[l], b_down[l])
        xp, k_p, v_p, _ = hybrid_layer(xp, None, None, *weights)
        kp.append(k_p)
        vp.append(v_p)
        xs, k_s, v_s, z_s = hybrid_layer(xs, cache_k[l], cache_v[l], *weights)
        ksm.append(k_s)
        vsm.append(v_s)
        zsm.append(z_s)
    y_prompt = rms_norm(xp, final_norm_g)
    y_sample = rms_norm(xs, final_norm_g)
    return (y_prompt, y_sample, jnp.stack(kp), jnp.stack(vp), jnp.stack(ksm), jnp.stack(vsm), jnp.stack(zsm))
```

```python
import functools

import jax
import jax.numpy as jnp
from jax import lax
from jax.experimental import pallas as pl
from jax.experimental.pallas import tpu as pltpu

F32 = jnp.float32
BF16 = jnp.bfloat16

D_MODEL = 1024
HEAD_DIM = 64
N_A_HEADS = 8
A_WIDTH = N_A_HEADS * HEAD_DIM
N_B_GROUPS = 8
B_GROUP_DIM = 64
B_WIDTH = N_B_GROUPS * B_GROUP_DIM
CHUNK = 64
MLP_CHUNK = 128
SB_SCALE = HEAD_DIM ** -0.5
N_EXPERTS = 32
TOP_K = 4
D_FF = D_MODEL
SWIGLU_LIMIT = 7.0
SWIGLU_ALPHA = 1.702
NORM_EPS = 1e-5

LANES = 128
HEADS_PER_TILE = LANES // HEAD_DIM
N_HEAD_TILES = A_WIDTH // LANES
ROUTER_LANES = 128
VMEM_LIMIT = 56 * 1024 * 1024
TOKEN_TILE = 256
ATTN_Q_TILE = 256
ATTN_K_TILE = 256
MOE_BLOCK = 512


def _params(sem):
    return pltpu.CompilerParams(dimension_semantics=sem, vmem_limit_bytes=VMEM_LIMIT)


def _split_bf16(t):
    hi = t.astype(BF16)
    lo = (t - hi.astype(F32)).astype(BF16)
    return hi, lo


def _dot(a, b):
    return jnp.dot(a, b, preferred_element_type=F32)


def _dot_nt(a, b):
    return lax.dot_general(a, b, (((1,), (1,)), ((), ())), preferred_element_type=F32)


def _rms(x, g):
    return x * lax.rsqrt(jnp.mean(x * x, axis=-1, keepdims=True) + NORM_EPS) * g


def _inproj_kernel(x_ref, g_ref, w_ref, lng_ref, lnb_ref, gavg_ref,
                   q_ref, k_ref, v_ref, kb_ref, vb_ref, u_ref, zn_ref, znb_ref):
    hn = _rms(x_ref[...], g_ref[...]).astype(BF16)

    def proj(sec):
        return _dot(hn, w_ref[:, sec * A_WIDTH:(sec + 1) * A_WIDTH])

    q_ref[...] = (proj(0) * SB_SCALE).astype(BF16)
    k = proj(1)
    k_ref[...] = k
    kb_ref[...] = k.astype(BF16)
    v = proj(2)
    v_ref[...] = v
    vb_ref[...] = v.astype(BF16)
    u_ref[...] = jax.nn.gelu(proj(3)).astype(BF16)
    z = jax.nn.gelu(proj(4))
    gavg = gavg_ref[...]

    def group_mean(t):
        hi, lo = _split_bf16(t)
        return _dot(hi, gavg) + _dot(lo, gavg)

    zc = z - group_mean(z)
    var = group_mean(zc * zc)
    zn = zc * lax.rsqrt(var + NORM_EPS) * lng_ref[...] + lnb_ref[...]
    zn_ref[...] = zn
    znb_ref[...] = zn.astype(BF16)


def _inproj(x, g, w_bf, lng, lnb, gavg):
    t = x.shape[0]
    tm = min(TOKEN_TILE, t)
    row = lambda w: pl.BlockSpec((tm, w), lambda i: (i, 0))
    full = lambda a: pl.BlockSpec(a.shape, lambda i: (0,) * a.ndim)
    sds = lambda w, dt: jax.ShapeDtypeStruct((t, w), dt)
    return pl.pallas_call(
        _inproj_kernel,
        grid=(t // tm,),
        in_specs=[row(D_MODEL), full(g), full(w_bf), full(lng), full(lnb), full(gavg)],
        out_specs=[row(A_WIDTH)] * 8,
        out_shape=[sds(A_WIDTH, BF16), sds(A_WIDTH, F32), sds(A_WIDTH, F32), sds(A_WIDTH, BF16),
                   sds(A_WIDTH, BF16), sds(B_WIDTH, BF16), sds(B_WIDTH, F32), sds(B_WIDTH, BF16)],
        compiler_params=_params(("parallel",)),
        name="inproj",
    )(x, g, w_bf, lng, lnb, gavg)


def _log_terms(z):
    soft = jnp.log1p(jnp.exp(-jnp.abs(z)))
    return jnp.minimum(z, 0.0) - soft, jnp.minimum(-z, 0.0) - soft


def _tri(n):
    j = lax.broadcasted_iota(jnp.int32, (n, n), 0)
    s = lax.broadcasted_iota(jnp.int32, (n, n), 1)
    return jnp.where(j > s, 1.0, 0.0).astype(BF16)


def _later_sum(log_keep, tri):
    hi, lo = _split_bf16(log_keep)
    return _dot(hi, tri) + _dot(lo, tri)


def _attn_kernel(q_ref, kd_ref, vd_ref, kp_ref, vp_ref, o_ref, *, n_past_fn, tk):
    tq = q_ref.shape[0]
    q = q_ref[...].astype(F32)
    lane = lax.broadcasted_iota(jnp.int32, (tq, LANES), 1)
    qs = [jnp.where((lane // HEAD_DIM) == h, q, 0.0).astype(BF16) for h in range(HEADS_PER_TILE)]

    kd = kd_ref[...].astype(BF16)
    vd = vd_ref[...].astype(BF16)
    t_pos = lax.broadcasted_iota(jnp.int32, (tq, tq), 0)
    s_pos = lax.broadcasted_iota(jnp.int32, (tq, tq), 1)
    visible = s_pos < t_pos
    tri_d = _tri(tq)
    accs, carries = [], []
    for qh in qs:
        log_beta, log_keep = _log_terms(_dot_nt(qh, kd))
        log_keep = jnp.where(visible, log_keep, 0.0)
        a = jnp.where(visible, jnp.exp(log_beta + _later_sum(log_keep, tri_d)), 0.0)
        accs.append(_dot(a.astype(BF16), vd))
        carries.append(jnp.sum(log_keep, axis=-1, keepdims=True))

    n_past = n_past_fn()
    tri_p = _tri(tk)

    def body(j, state):
        accs, carries = state
        start = pl.multiple_of((n_past - 1 - j) * tk, tk)
        kp = kp_ref[pl.ds(start, tk), :].astype(BF16)
        vp = vp_ref[pl.ds(start, tk), :].astype(BF16)
        new_accs, new_carries = [], []
        for qh, acc, carry in zip(qs, accs, carries):
            log_beta, log_keep = _log_terms(_dot_nt(qh, kp))
            a = jnp.exp(log_beta + _later_sum(log_keep, tri_p) + carry)
            new_accs.append(acc + _dot(a.astype(BF16), vp))
            new_carries.append(carry + jnp.sum(log_keep, axis=-1, keepdims=True))
        return tuple(new_accs), tuple(new_carries)

    accs, _ = lax.fori_loop(0, n_past, body, (tuple(accs), tuple(carries)))
    out = accs[0]
    for h in range(1, HEADS_PER_TILE):
        out = jnp.where((lane // HEAD_DIM) == h, accs[h], out)
    o_ref[...] = out


def _attn_prompt(q_bf, k_bf, v_bf):
    b, s, _ = q_bf.shape
    tq = min(ATTN_Q_TILE, s)
    tk = tq
    blk = pl.BlockSpec((None, tq, LANES), lambda bi, hp, i: (bi, i, hp))
    seq = pl.BlockSpec((None, s, LANES), lambda bi, hp, i: (bi, 0, hp))
    kern = functools.partial(_attn_kernel, n_past_fn=lambda: pl.program_id(2) * (tq // tk), tk=tk)
    return pl.pallas_call(
        kern,
        grid=(b, N_HEAD_TILES, s // tq),
        in_specs=[blk, blk, blk, seq, seq],
        out_specs=blk,
        out_shape=jax.ShapeDtypeStruct((b, s, A_WIDTH), F32),
        compiler_params=_params(("parallel", "parallel", "arbitrary")),
        name="attn_prompt",
    )(q_bf, k_bf, v_bf, k_bf, v_bf)


def _attn_sample(q_bf, k_bf, v_bf, cache_k, cache_v, layer):
    b, n, _ = q_bf.shape
    p = cache_k.shape[2]
    tk = min(ATTN_K_TILE, p)
    blk = pl.BlockSpec((None, n, LANES), lambda bi, hp: (bi, 0, hp))
    past = pl.BlockSpec((None, None, p, LANES), lambda bi, hp: (layer, bi, 0, hp))
    kern = functools.partial(_attn_kernel, n_past_fn=lambda: p // tk, tk=tk)
    return pl.pallas_call(
        kern,
        grid=(b, N_HEAD_TILES),
        in_specs=[blk, blk, blk, past, past],
        out_specs=blk,
        out_shape=jax.ShapeDtypeStruct((b, n, A_WIDTH), F32),
        compiler_params=_params(("parallel", "parallel")),
        name="attn_sample",
    )(q_bf, k_bf, v_bf, cache_k, cache_v)


def _mixout_kernel(x_ref, oa_ref, u_ref, znb_ref, wsp_ref, bsp_ref, ga_ref, gb_ref, wout_ref,
                   lnf_ref, wrh_ref, wrl_ref, br_ref, x1_ref, h2_ref, lg_ref, mix_ref):
    tm = x_ref.shape[0]
    chunk = wsp_ref.shape[1]
    group = lax.broadcasted_iota(jnp.int32, (chunk, B_WIDTH), 1) // B_GROUP_DIM
    for c in range(tm // chunk):
        zc = znb_ref[c * chunk:(c + 1) * chunk, :]
        m = jnp.zeros((chunk, B_WIDTH), F32)
        for g in range(N_B_GROUPS):
            m = jnp.where(group == g, _dot(wsp_ref[g], zc), m)
        mix_ref[c * chunk:(c + 1) * chunk, :] = m + bsp_ref[...]
    ob = u_ref[...].astype(F32) * mix_ref[...]
    ya = _rms(oa_ref[...], ga_ref[...]).astype(BF16)
    yb = _rms(ob, gb_ref[...]).astype(BF16)
    x1 = x_ref[...] + _dot(ya, wout_ref[:A_WIDTH, :]) + _dot(yb, wout_ref[A_WIDTH:, :])
    x1_ref[...] = x1
    h2 = _rms(x1, lnf_ref[...])
    hi, lo = _split_bf16(h2)
    h2_ref[...] = hi
    wrh = wrh_ref[...]
    lg_ref[...] = _dot(hi, wrh) + _dot(lo, wrh) + _dot(hi, wrl_ref[...]) + br_ref[...]


def _mixout(x, oa, u, znb, wsp, bsp, ga, gb, wout_bf, lnf, wrh, wrl, br):
    t = x.shape[0]
    tm = min(TOKEN_TILE, t)
    row = lambda w: pl.BlockSpec((tm, w), lambda i: (i, 0))
    full = lambda a: pl.BlockSpec(a.shape, lambda i: (0,) * a.ndim)
    return pl.pallas_call(
        _mixout_kernel,
        grid=(t // tm,),
        in_specs=[row(D_MODEL), row(A_WIDTH), row(B_WIDTH), row(B_WIDTH), full(wsp), full(bsp),
                  full(ga), full(gb), full(wout_bf), full(lnf), full(wrh), full(wrl), full(br)],
        out_specs=[row(D_MODEL), row(D_MODEL), row(ROUTER_LANES)],
        out_shape=[jax.ShapeDtypeStruct((t, D_MODEL), F32), jax.ShapeDtypeStruct((t, D_MODEL), BF16),
                   jax.ShapeDtypeStruct((t, ROUTER_LANES), F32)],
        scratch_shapes=[pltpu.VMEM((tm, B_WIDTH), F32)],
        compiler_params=_params(("parallel",)),
        name="mixout",
    )(x, oa, u, znb, wsp, bsp, ga, gb, wout_bf, lnf, wrh, wrl, br)


def _moe_kernel(bexp_ref, bval_ref, xs_ref, wgu_ref, bgu_ref, wd_ref, bd_ref, o_ref):
    i = pl.program_id(0)

    @pl.when(bval_ref[i] > 0)
    def _():
        xs = xs_ref[...]
        gate = _dot(xs, wgu_ref[:, :D_FF]) + bgu_ref[:, :D_FF]
        lin = _dot(xs, wgu_ref[:, D_FF:]) + bgu_ref[:, D_FF:]
        gate = jnp.minimum(gate, SWIGLU_LIMIT)
        lin = jnp.clip(lin, -SWIGLU_LIMIT, SWIGLU_LIMIT)
        act = gate * jax.nn.sigmoid(SWIGLU_ALPHA * gate) * (lin + 1.0)
        o_ref[...] = _dot(act.astype(BF16), wd_ref[...]) + bd_ref[...]

    @pl.when(bval_ref[i] == 0)
    def _():
        o_ref[...] = jnp.zeros_like(o_ref)


def _moe(block_exp, block_valid, xs, wgu_bf, bgu, wd_bf, bd, layer):
    n_slots = xs.shape[0]
    bm = MOE_BLOCK
    grid_spec = pltpu.PrefetchScalarGridSpec(
        num_scalar_prefetch=2,
        grid=(n_slots // bm,),
        in_specs=[
            pl.BlockSpec((bm, D_MODEL), lambda i, be, bv: (i, 0)),
            pl.BlockSpec((None, None, D_MODEL, 2 * D_FF), lambda i, be, bv: (layer, be[i], 0, 0)),
            pl.BlockSpec((None, None, 1, 2 * D_FF), lambda i, be, bv: (layer, be[i], 0, 0)),
            pl.BlockSpec((None, None, D_FF, D_MODEL), lambda i, be, bv: (layer, be[i], 0, 0)),
            pl.BlockSpec((None, None, 1, D_MODEL), lambda i, be, bv: (layer, be[i], 0, 0)),
        ],
        out_specs=pl.BlockSpec((bm, D_MODEL), lambda i, be, bv: (i, 0)),
    )
    return pl.pallas_call(
        _moe_kernel,
        grid_spec=grid_spec,
        out_shape=jax.ShapeDtypeStruct((n_slots, D_MODEL), F32),
        compiler_params=_params(("arbitrary",)),
        name="moe",
    )(block_exp, block_valid, xs, wgu_bf, bgu, wd_bf, bd)


def _combine_kernel(x_ref, rows_ref, gates_ref, g_ref, y_ref, *, final):
    y = x_ref[...]
    gates = gates_ref[...]
    for k in range(TOP_K):
        y = y + gates[:, k:k + 1] * rows_ref[k]
    y_ref[...] = _rms(y, g_ref[...]) if final else y


def _combine(x1, rows, gates, g, final):
    t = x1.shape[0]
    tm = min(TOKEN_TILE, t)
    return pl.pallas_call(
        functools.partial(_combine_kernel, final=final),
        grid=(t // tm,),
        in_specs=[pl.BlockSpec((tm, D_MODEL), lambda i: (i, 0)),
                  pl.BlockSpec((TOP_K, tm, D_MODEL), lambda i: (0, i, 0)),
                  pl.BlockSpec((tm, ROUTER_LANES), lambda i: (i, 0)),
                  pl.BlockSpec(g.shape, lambda i: (0, 0))],
        out_specs=pl.BlockSpec((tm, D_MODEL), lambda i: (i, 0)),
        out_shape=jax.ShapeDtypeStruct((t, D_MODEL), F32),
        compiler_params=_params(("parallel",)),
        name="combine",
    )(x1, rows, gates, g)


def _route(logits):
    t = logits.shape[0]
    top_val, top_exp = lax.top_k(logits, TOP_K)
    gates = jax.nn.softmax(top_val, axis=-1)
    n_rows = t * TOP_K
    flat_exp = top_exp.reshape(-1)
    onehot = (flat_exp[:, None] == jnp.arange(N_EXPERTS, dtype=jnp.int32)[None, :]).astype(jnp.int32)
    csum = jnp.cumsum(onehot, axis=0)
    counts = csum[-1]
    rank = jnp.take_along_axis(csum, flat_exp[:, None], axis=1)[:, 0] - 1
    padded = (counts + MOE_BLOCK - 1) // MOE_BLOCK * MOE_BLOCK
    pend = jnp.cumsum(padded)
    pstart = pend - padded
    slot = pstart[flat_exp] + rank
    n_blocks = -(-n_rows // MOE_BLOCK) + N_EXPERTS
    n_slots = n_blocks * MOE_BLOCK
    flat_tok = jnp.arange(n_rows, dtype=jnp.int32) // TOP_K
    slot_tok = jnp.full((n_slots,), t, jnp.int32).at[slot].set(flat_tok)
    bstart = jnp.arange(n_blocks, dtype=jnp.int32) * MOE_BLOCK
    block_exp = jnp.minimum(jnp.searchsorted(pend, bstart, side='right'), N_EXPERTS - 1).astype(jnp.int32)
    block_valid = (bstart < pend[-1]).astype(jnp.int32)
    return slot_tok, slot.reshape(t, TOP_K), gates, block_exp, block_valid


def _layer(l, xp, xs, cache_k, cache_v, w):
    bsz, seq, _ = xp.shape
    dbsz, dseq, _ = xs.shape
    tp, ts = bsz * seq, dbsz * dseq

    def dense(x2, n_stream, n_seq, chunk, attn_fn):
        q, k, v, kb, vb, u, zn, znb = _inproj(x2, w['ln_mix_g'][l], w['w_in'][l], w['sgu_ln_g'][l],
                                              w['sgu_ln_b'][l], w['gavg'])
        r3 = lambda a: a.reshape(n_stream, n_seq, A_WIDTH)
        oa = attn_fn(r3(q), r3(kb), r3(vb)).reshape(n_stream * n_seq, A_WIDTH)
        wsp = w['w_spatial'][l][:, :chunk, :chunk]
        bsp = w['b_spatial'][l][:chunk]
        x1, h2, lg = _mixout(x2, oa, u, znb, wsp, bsp, w['out_norm_a_g'][l], w['out_norm_b_g'][l],
                             w['w_out'][l], w['ln_ffn_g'][l], w['wr_hi'][l], w['wr_lo'][l], w['b_router'][l])
        return x1, h2, lg, k, v, zn

    x1p, h2p, lgp, kp, vp, _ = dense(xp.reshape(tp, D_MODEL), bsz, seq, MLP_CHUNK, _attn_prompt)
    x1s, h2s, lgs, ks, vs, zs = dense(xs.reshape(ts, D_MODEL), dbsz, dseq, dseq,
                                      lambda q, k, v: _attn_sample(q, k, v, cache_k, cache_v, l))

    t = tp + ts
    h2 = jnp.concatenate([h2p, h2s, jnp.zeros((1, D_MODEL), BF16)], axis=0)
    logits = jnp.concatenate([lgp, lgs], axis=0)[:, :N_EXPERTS]
    slot_tok, tok_slot, gates, block_exp, block_valid = _route(logits)
    x_sorted = jnp.take(h2, slot_tok, axis=0)
    out = _moe(block_exp, block_valid, x_sorted, w['w_gate_up'], w['b_gate_up'], w['w_down'], w['b_down'], l)
    rows = jnp.take(out, tok_slot.T, axis=0)
    gates = jnp.pad(gates, ((0, 0), (0, ROUTER_LANES - TOP_K)))
    final = l == w['depth'] - 1
    yp = _combine(x1p, rows[:, :tp], gates[:tp], w['final_norm_g'], final)
    ys = _combine(x1s, rows[:, tp:], gates[tp:], w['final_norm_g'], final)
    to_heads = lambda a, n_stream, n_seq: a.reshape(n_stream, n_seq, N_A_HEADS, HEAD_DIM)
    return (yp.reshape(bsz, seq, D_MODEL), ys.reshape(dbsz, dseq, D_MODEL),
            to_heads(kp, bsz, seq), to_heads(vp, bsz, seq), to_heads(ks, dbsz, dseq), to_heads(vs, dbsz, dseq),
            zs.reshape(dbsz, dseq, N_B_GROUPS, B_GROUP_DIM))


def _prepare(ln_mix_g, w_in, sgu_ln_g, sgu_ln_b, w_spatial, b_spatial, out_norm_a_g, out_norm_b_g, w_out,
             ln_ffn_g, w_router, b_router, w_gate_up, b_gate_up, w_down, b_down, final_norm_g):
    depth = w_in.shape[0]
    row = lambda a: a.reshape(depth, 1, -1)
    blk = jnp.arange(MLP_CHUNK, dtype=jnp.int32) // CHUNK
    mask = blk[None, :] <= blk[:, None]
    grp = jnp.arange(B_WIDTH, dtype=jnp.int32) // B_GROUP_DIM
    gavg = jnp.where(grp[:, None] == grp[None, :], 1.0 / B_GROUP_DIM, 0.0).astype(BF16)
    wr = jnp.pad(w_router, ((0, 0), (0, 0), (0, ROUTER_LANES - N_EXPERTS)))
    wr_hi = wr.astype(BF16)
    wr_lo = (wr - wr_hi.astype(F32)).astype(BF16)
    return dict(
        depth=depth,
        ln_mix_g=row(ln_mix_g), w_in=w_in.astype(BF16), sgu_ln_g=row(sgu_ln_g), sgu_ln_b=row(sgu_ln_b),
        gavg=gavg,
        w_spatial=jnp.where(mask[None, None], w_spatial, 0.0).astype(BF16),
        b_spatial=jnp.repeat(jnp.swapaxes(b_spatial, 1, 2), B_GROUP_DIM, axis=2),
        out_norm_a_g=row(out_norm_a_g), out_norm_b_g=row(out_norm_b_g), w_out=w_out.astype(BF16),
        ln_ffn_g=row(ln_ffn_g), wr_hi=wr_hi, wr_lo=wr_lo,
        b_router=row(jnp.pad(b_router, ((0, 0), (0, ROUTER_LANES - N_EXPERTS)))),
        w_gate_up=w_gate_up.astype(BF16), b_gate_up=b_gate_up[:, :, None, :],
        w_down=w_down.astype(BF16), b_down=b_down[:, :, None, :],
        final_norm_g=final_norm_g.reshape(1, -1),
    )


def kernel(x_prompt, x_sample, cache_k, cache_v, ln_mix_g, w_in, sgu_ln_g, sgu_ln_b, w_spatial, b_spatial,
           out_norm_a_g, out_norm_b_g, w_out, ln_ffn_g, w_router, b_router, w_gate_up, b_gate_up, w_down,
           b_down, final_norm_g):
    w = _prepare(ln_mix_g, w_in, sgu_ln_g, sgu_ln_b, w_spatial, b_spatial, out_norm_a_g, out_norm_b_g, w_out,
                 ln_ffn_g, w_router, b_router, w_gate_up, b_gate_up, w_down, b_down, final_norm_g)
    depth, dbsz, past = cache_k.shape[:3]
    ck = cache_k.reshape(depth, dbsz, past, A_WIDTH)
    cv = cache_v.reshape(depth, dbsz, past, A_WIDTH)
    xp, xs = x_prompt, x_sample
    kps, vps, kss, vss, zss = [], [], [], [], []
    for l in range(depth):
        xp, xs, kp, vp, ks, vs, zs = _layer(l, xp, xs, ck, cv, w)
        kps.append(kp)
        vps.append(vp)
        kss.append(ks)
        vss.append(vs)
        zss.append(zs)
    return (xp, xs, jnp.stack(kps), jnp.stack(vps), jnp.stack(kss), jnp.stack(vss), jnp.stack(zss))
```

```python
import functools

import jax
import jax.numpy as jnp
from jax import lax
from jax.experimental import pallas as pl
from jax.experimental.pallas import tpu as pltpu

F32 = jnp.float32
BF16 = jnp.bfloat16

D_MODEL = 1024
HEAD_DIM = 64
N_A_HEADS = 8
A_WIDTH = N_A_HEADS * HEAD_DIM
N_B_GROUPS = 8
B_GROUP_DIM = 64
B_WIDTH = N_B_GROUPS * B_GROUP_DIM
CHUNK = 64
MLP_CHUNK = 128
SB_SCALE = HEAD_DIM ** -0.5
N_EXPERTS = 32
TOP_K = 4
D_FF = D_MODEL
SWIGLU_LIMIT = 7.0
SWIGLU_ALPHA = 1.702
NORM_EPS = 1e-5

LANES = 128
HEADS_PER_TILE = LANES // HEAD_DIM
N_HEAD_TILES = A_WIDTH // LANES
ROUTER_LANES = 128
VMEM_LIMIT = 56 * 1024 * 1024
TOKEN_TILE = 256
ATTN_Q_TILE = 256
ATTN_K_TILE = 256
MOE_BLOCK = 512
WEIGHT_CAST_ROWS = 128
ZERO_WEIGHT_LOG = -110.0


def _params(sem):
    return pltpu.CompilerParams(dimension_semantics=sem, vmem_limit_bytes=VMEM_LIMIT)


def _split_bf16(t):
    hi = t.astype(BF16)
    lo = (t - hi.astype(F32)).astype(BF16)
    return hi, lo


def _dot(a, b):
    return jnp.dot(a, b, preferred_element_type=F32)


def _dot_nt(a, b):
    return lax.dot_general(a, b, (((1,), (1,)), ((), ())), preferred_element_type=F32)


def _rms(x, g):
    return x * lax.rsqrt(jnp.mean(x * x, axis=-1, keepdims=True) + NORM_EPS) * g


def _inproj_kernel(x_ref, g_ref, w_ref, lng_ref, lnb_ref, gavg_ref,
                   q_ref, k_ref, v_ref, kb_ref, vb_ref, u_ref, zn_ref, znb_ref):
    hn = _rms(x_ref[...], g_ref[...]).astype(BF16)

    def proj(sec):
        return _dot(hn, w_ref[:, sec * A_WIDTH:(sec + 1) * A_WIDTH])

    q_ref[...] = (proj(0) * SB_SCALE).astype(BF16)
    k = proj(1)
    k_ref[...] = k
    kb_ref[...] = k.astype(BF16)
    v = proj(2)
    v_ref[...] = v
    vb_ref[...] = v.astype(BF16)
    u_ref[...] = jax.nn.gelu(proj(3)).astype(BF16)
    z = jax.nn.gelu(proj(4))
    gavg = gavg_ref[...]

    def group_mean(t):
        hi, lo = _split_bf16(t)
        return _dot(hi, gavg) + _dot(lo, gavg)

    zc = z - group_mean(z)
    var = group_mean(zc * zc)
    zn = zc * lax.rsqrt(var + NORM_EPS) * lng_ref[...] + lnb_ref[...]
    zn_ref[...] = zn
    znb_ref[...] = zn.astype(BF16)


def _inproj(x, g, w_bf, lng, lnb, gavg):
    t = x.shape[0]
    tm = min(TOKEN_TILE, t)
    row = lambda w: pl.BlockSpec((tm, w), lambda i: (i, 0))
    full = lambda a: pl.BlockSpec(a.shape, lambda i: (0,) * a.ndim)
    sds = lambda w, dt: jax.ShapeDtypeStruct((t, w), dt)
    return pl.pallas_call(
        _inproj_kernel,
        grid=(t // tm,),
        in_specs=[row(D_MODEL), full(g), full(w_bf), full(lng), full(lnb), full(gavg)],
        out_specs=[row(A_WIDTH)] * 8,
        out_shape=[sds(A_WIDTH, BF16), sds(A_WIDTH, F32), sds(A_WIDTH, F32), sds(A_WIDTH, BF16),
                   sds(A_WIDTH, BF16), sds(B_WIDTH, BF16), sds(B_WIDTH, F32), sds(B_WIDTH, BF16)],
        compiler_params=_params(("parallel",)),
        name="inproj",
    )(x, g, w_bf, lng, lnb, gavg)


def _log_terms(z):
    soft = jnp.log1p(jnp.exp(-jnp.abs(z)))
    return jnp.minimum(z, 0.0) - soft, jnp.minimum(-z, 0.0) - soft


def _tri(n):
    j = lax.broadcasted_iota(jnp.int32, (n, n), 0)
    s = lax.broadcasted_iota(jnp.int32, (n, n), 1)
    return jnp.where(j > s, 1.0, 0.0).astype(BF16)


def _later_sum(log_keep, tri):
    hi, lo = _split_bf16(log_keep)
    return _dot(hi, tri) + _dot(lo, tri)


def _attn_kernel(q_ref, kd_ref, vd_ref, kp_ref, vp_ref, o_ref, *, n_past_fn, tk):
    tq = q_ref.shape[0]
    q = q_ref[...].astype(F32)
    lane = lax.broadcasted_iota(jnp.int32, (tq, LANES), 1)
    qs = [jnp.where((lane // HEAD_DIM) == h, q, 0.0).astype(BF16) for h in range(HEADS_PER_TILE)]

    kd = kd_ref[...].astype(BF16)
    vd = vd_ref[...].astype(BF16)
    t_pos = lax.broadcasted_iota(jnp.int32, (tq, tq), 0)
    s_pos = lax.broadcasted_iota(jnp.int32, (tq, tq), 1)
    visible = s_pos < t_pos
    tri_d = _tri(tq)
    accs, carries = [], []
    for qh in qs:
        log_beta, log_keep = _log_terms(_dot_nt(qh, kd))
        log_keep = jnp.where(visible, log_keep, 0.0)
        a = jnp.where(visible, jnp.exp(log_beta + _later_sum(log_keep, tri_d)), 0.0)
        accs.append(_dot(a.astype(BF16), vd))
        carries.append(jnp.sum(log_keep, axis=-1, keepdims=True))

    n_past = n_past_fn()
    tri_p = _tri(tk)

    def carry_max(carries):
        return functools.reduce(jnp.maximum, [jnp.max(c) for c in carries])

    def cond(state):
        j, _, _, cmax = state
        return jnp.logical_and(j < n_past, cmax > ZERO_WEIGHT_LOG)

    def body(state):
        j, accs, carries, _ = state
        start = pl.multiple_of((n_past - 1 - j) * tk, tk)
        kp = kp_ref[pl.ds(start, tk), :].astype(BF16)
        vp = vp_ref[pl.ds(start, tk), :].astype(BF16)
        new_accs, new_carries = [], []
        for qh, acc, carry in zip(qs, accs, carries):
            log_beta, log_keep = _log_terms(_dot_nt(qh, kp))
            a = jnp.exp(log_beta + _later_sum(log_keep, tri_p) + carry)
            new_accs.append(acc + _dot(a.astype(BF16), vp))
            new_carries.append(carry + jnp.sum(log_keep, axis=-1, keepdims=True))
        return j + 1, tuple(new_accs), tuple(new_carries), carry_max(new_carries)

    _, accs, _, _ = lax.while_loop(cond, body, (jnp.int32(0), tuple(accs), tuple(carries), carry_max(carries)))
    out = accs[0]
    for h in range(1, HEADS_PER_TILE):
        out = jnp.where((lane // HEAD_DIM) == h, accs[h], out)
    o_ref[...] = out


def _attn_prompt(q_bf, k_bf, v_bf):
    b, s, _ = q_bf.shape
    tq = min(ATTN_Q_TILE, s)
    tk = tq
    blk = pl.BlockSpec((None, tq, LANES), lambda bi, hp, i: (bi, i, hp))
    seq = pl.BlockSpec((None, s, LANES), lambda bi, hp, i: (bi, 0, hp))
    kern = functools.partial(_attn_kernel, n_past_fn=lambda: pl.program_id(2) * (tq // tk), tk=tk)
    return pl.pallas_call(
        kern,
        grid=(b, N_HEAD_TILES, s // tq),
        in_specs=[blk, blk, blk, seq, seq],
        out_specs=blk,
        out_shape=jax.ShapeDtypeStruct((b, s, A_WIDTH), F32),
        compiler_params=_params(("parallel", "parallel", "arbitrary")),
        name="attn_prompt",
    )(q_bf, k_bf, v_bf, k_bf, v_bf)


def _attn_sample(q_bf, k_bf, v_bf, cache_k, cache_v, layer):
    b, n, _ = q_bf.shape
    p = cache_k.shape[2]
    tk = min(ATTN_K_TILE, p)
    blk = pl.BlockSpec((None, n, LANES), lambda bi, hp: (bi, 0, hp))
    past = pl.BlockSpec((None, None, p, LANES), lambda bi, hp: (layer, bi, 0, hp))
    kern = functools.partial(_attn_kernel, n_past_fn=lambda: p // tk, tk=tk)
    return pl.pallas_call(
        kern,
        grid=(b, N_HEAD_TILES),
        in_specs=[blk, blk, blk, past, past],
        out_specs=blk,
        out_shape=jax.ShapeDtypeStruct((b, n, A_WIDTH), F32),
        compiler_params=_params(("parallel", "parallel")),
        name="attn_sample",
    )(q_bf, k_bf, v_bf, cache_k, cache_v)


def _mixout_kernel(x_ref, oa_ref, u_ref, znb_ref, wsp_ref, bsp_ref, ga_ref, gb_ref, wout_ref,
                   lnf_ref, wrh_ref, wrl_ref, br_ref, x1_ref, h2_ref, lg_ref, mix_ref):
    tm = x_ref.shape[0]
    chunk = wsp_ref.shape[1]
    group = lax.broadcasted_iota(jnp.int32, (chunk, B_WIDTH), 1) // B_GROUP_DIM
    for c in range(tm // chunk):
        zc = znb_ref[c * chunk:(c + 1) * chunk, :]
        m = jnp.zeros((chunk, B_WIDTH), F32)
        for g in range(N_B_GROUPS):
            m = jnp.where(group == g, _dot(wsp_ref[g], zc), m)
        mix_ref[c * chunk:(c + 1) * chunk, :] = m + bsp_ref[...]
    ob = u_ref[...].astype(F32) * mix_ref[...]
    ya = _rms(oa_ref[...], ga_ref[...]).astype(BF16)
    yb = _rms(ob, gb_ref[...]).astype(BF16)
    x1 = x_ref[...] + _dot(ya, wout_ref[:A_WIDTH, :]) + _dot(yb, wout_ref[A_WIDTH:, :])
    x1_ref[...] = x1
    h2 = _rms(x1, lnf_ref[...])
    hi, lo = _split_bf16(h2)
    h2_ref[...] = hi
    wrh = wrh_ref[...]
    lg_ref[...] = _dot(hi, wrh) + _dot(lo, wrh) + _dot(hi, wrl_ref[...]) + br_ref[...]


def _mixout(x, oa, u, znb, wsp, bsp, ga, gb, wout_bf, lnf, wrh, wrl, br):
    t = x.shape[0]
    tm = min(TOKEN_TILE, t)
    row = lambda w: pl.BlockSpec((tm, w), lambda i: (i, 0))
    full = lambda a: pl.BlockSpec(a.shape, lambda i: (0,) * a.ndim)
    return pl.pallas_call(
        _mixout_kernel,
        grid=(t // tm,),
        in_specs=[row(D_MODEL), row(A_WIDTH), row(B_WIDTH), row(B_WIDTH), full(wsp), full(bsp),
                  full(ga), full(gb), full(wout_bf), full(lnf), full(wrh), full(wrl), full(br)],
        out_specs=[row(D_MODEL), row(D_MODEL), row(ROUTER_LANES)],
        out_shape=[jax.ShapeDtypeStruct((t, D_MODEL), F32), jax.ShapeDtypeStruct((t, D_MODEL), BF16),
                   jax.ShapeDtypeStruct((t, ROUTER_LANES), F32)],
        scratch_shapes=[pltpu.VMEM((tm, B_WIDTH), F32)],
        compiler_params=_params(("parallel",)),
        name="mixout",
    )(x, oa, u, znb, wsp, bsp, ga, gb, wout_bf, lnf, wrh, wrl, br)


def _moe_kernel(bexp_ref, bval_ref, xs_ref, wgu_ref, bgu_ref, wd_ref, bd_ref, o_ref, wgu_bf, wd_bf):
    i = pl.program_id(0)
    valid = bval_ref[i] > 0
    new_expert = jnp.logical_or(i == 0, bexp_ref[i] != bexp_ref[jnp.maximum(i - 1, 0)])

    @pl.when(jnp.logical_and(valid, new_expert))
    def _():
        def cast_rows(r, _):
            rows = pl.ds(pl.multiple_of(r * WEIGHT_CAST_ROWS, WEIGHT_CAST_ROWS), WEIGHT_CAST_ROWS)
            wgu_bf[rows, :] = wgu_ref[rows, :].astype(BF16)
            wd_bf[rows, :] = wd_ref[rows, :].astype(BF16)
            return 0
        lax.fori_loop(0, D_MODEL // WEIGHT_CAST_ROWS, cast_rows, 0)

    @pl.when(valid)
    def _():
        xs = xs_ref[...]
        gate = _dot(xs, wgu_bf[:, :D_FF]) + bgu_ref[:, :D_FF]
        lin = _dot(xs, wgu_bf[:, D_FF:]) + bgu_ref[:, D_FF:]
        gate = jnp.minimum(gate, SWIGLU_LIMIT)
        lin = jnp.clip(lin, -SWIGLU_LIMIT, SWIGLU_LIMIT)
        act = gate * jax.nn.sigmoid(SWIGLU_ALPHA * gate) * (lin + 1.0)
        o_ref[...] = _dot(act.astype(BF16), wd_bf[...]) + bd_ref[...]

    @pl.when(bval_ref[i] == 0)
    def _():
        o_ref[...] = jnp.zeros_like(o_ref)


def _moe(block_exp, block_valid, xs, wgu, bgu, wd, bd, layer):
    n_slots = xs.shape[0]
    bm = MOE_BLOCK
    grid_spec = pltpu.PrefetchScalarGridSpec(
        num_scalar_prefetch=2,
        grid=(n_slots // bm,),
        in_specs=[
            pl.BlockSpec((bm, D_MODEL), lambda i, be, bv: (i, 0)),
            pl.BlockSpec((None, None, D_MODEL, 2 * D_FF), lambda i, be, bv: (layer, be[i], 0, 0)),
            pl.BlockSpec((None, None, 1, 2 * D_FF), lambda i, be, bv: (layer, be[i], 0, 0)),
            pl.BlockSpec((None, None, D_FF, D_MODEL), lambda i, be, bv: (layer, be[i], 0, 0)),
            pl.BlockSpec((None, None, 1, D_MODEL), lambda i, be, bv: (layer, be[i], 0, 0)),
        ],
        out_specs=pl.BlockSpec((bm, D_MODEL), lambda i, be, bv: (i, 0)),
        scratch_shapes=[pltpu.VMEM((D_MODEL, 2 * D_FF), BF16), pltpu.VMEM((D_FF, D_MODEL), BF16)],
    )
    return pl.pallas_call(
        _moe_kernel,
        grid_spec=grid_spec,
        out_shape=jax.ShapeDtypeStruct((n_slots, D_MODEL), F32),
        compiler_params=_params(("arbitrary",)),
        name="moe",
    )(block_exp, block_valid, xs, wgu, bgu, wd, bd)


def _combine_kernel(x_ref, rows_ref, gates_ref, g_ref, y_ref, *, final):
    y = x_ref[...]
    gates = gates_ref[...]
    for k in range(TOP_K):
        y = y + gates[:, k:k + 1] * rows_ref[k]
    y_ref[...] = _rms(y, g_ref[...]) if final else y


def _combine(x1, rows, gates, g, final, row_offset):
    t = x1.shape[0]
    tm = min(TOKEN_TILE, t)
    off = row_offset // tm
    return pl.pallas_call(
        functools.partial(_combine_kernel, final=final),
        grid=(t // tm,),
        in_specs=[pl.BlockSpec((tm, D_MODEL), lambda i: (i, 0)),
                  pl.BlockSpec((TOP_K, tm, D_MODEL), lambda i: (0, i + off, 0)),
                  pl.BlockSpec((tm, ROUTER_LANES), lambda i: (i + off, 0)),
                  pl.BlockSpec(g.shape, lambda i: (0, 0))],
        out_specs=pl.BlockSpec((tm, D_MODEL), lambda i: (i, 0)),
        out_shape=jax.ShapeDtypeStruct((t, D_MODEL), F32),
        compiler_params=_params(("parallel",)),
        name="combine",
    )(x1, rows, gates, g)


def _route(logits):
    t = logits.shape[0]
    top_val, top_exp = lax.top_k(logits, TOP_K)
    gates = jax.nn.softmax(top_val, axis=-1)
    n_rows = t * TOP_K
    flat_exp = top_exp.reshape(-1)
    onehot = (flat_exp[:, None] == jnp.arange(N_EXPERTS, dtype=jnp.int32)[None, :]).astype(jnp.int32)
    csum = jnp.cumsum(onehot, axis=0)
    counts = csum[-1]
    rank = jnp.take_along_axis(csum, flat_exp[:, None], axis=1)[:, 0] - 1
    padded = (counts + MOE_BLOCK - 1) // MOE_BLOCK * MOE_BLOCK
    pend = jnp.cumsum(padded)
    pstart = pend - padded
    slot = pstart[flat_exp] + rank
    n_blocks = -(-n_rows // MOE_BLOCK) + N_EXPERTS
    n_slots = n_blocks * MOE_BLOCK
    flat_tok = jnp.arange(n_rows, dtype=jnp.int32) // TOP_K
    slot_tok = jnp.full((n_slots,), t, jnp.int32).at[slot].set(flat_tok)
    bstart = jnp.arange(n_blocks, dtype=jnp.int32) * MOE_BLOCK
    block_exp = jnp.minimum(jnp.sum((pend[None, :] <= bstart[:, None]).astype(jnp.int32), axis=1), N_EXPERTS - 1)
    block_valid = (bstart < pend[-1]).astype(jnp.int32)
    return slot_tok, slot.reshape(t, TOP_K), gates, block_exp, block_valid


def _layer(l, xp, xs, cache_k, cache_v, w):
    bsz, seq, _ = xp.shape
    dbsz, dseq, _ = xs.shape
    tp, ts = bsz * seq, dbsz * dseq

    def dense(x2, n_stream, n_seq, chunk, attn_fn):
        q, k, v, kb, vb, u, zn, znb = _inproj(x2, w['ln_mix_g'][l], w['w_in'][l], w['sgu_ln_g'][l],
                                              w['sgu_ln_b'][l], w['gavg'])
        r3 = lambda a: a.reshape(n_stream, n_seq, A_WIDTH)
        oa = attn_fn(r3(q), r3(kb), r3(vb)).reshape(n_stream * n_seq, A_WIDTH)
        wsp = w['w_spatial'][l][:, :chunk, :chunk]
        bsp = w['b_spatial'][l][:chunk]
        x1, h2, lg = _mixout(x2, oa, u, znb, wsp, bsp, w['out_norm_a_g'][l], w['out_norm_b_g'][l],
                             w['w_out'][l], w['ln_ffn_g'][l], w['wr_hi'][l], w['wr_lo'][l], w['b_router'][l])
        return x1, h2, lg, k, v, zn

    x1p, h2p, lgp, kp, vp, _ = dense(xp.reshape(tp, D_MODEL), bsz, seq, MLP_CHUNK, _attn_prompt)
    x1s, h2s, lgs, ks, vs, zs = dense(xs.reshape(ts, D_MODEL), dbsz, dseq, dseq,
                                      lambda q, k, v: _attn_sample(q, k, v, cache_k, cache_v, l))

    t = tp + ts
    h2 = jnp.concatenate([h2p, h2s, jnp.zeros((1, D_MODEL), BF16)], axis=0)
    logits = jnp.concatenate([lgp, lgs], axis=0)[:, :N_EXPERTS]
    slot_tok, tok_slot, gates, block_exp, block_valid = _route(logits)
    x_sorted = jnp.take(h2, slot_tok, axis=0)
    out = _moe(block_exp, block_valid, x_sorted, w['w_gate_up'], w['b_gate_up'], w['w_down'], w['b_down'], l)
    rows = jnp.take(out, tok_slot.T, axis=0)
    gates = jnp.pad(gates, ((0, 0), (0, ROUTER_LANES - TOP_K)))
    final = l == w['depth'] - 1
    yp = _combine(x1p, rows, gates, w['final_norm_g'], final, 0)
    ys = _combine(x1s, rows, gates, w['final_norm_g'], final, tp)
    to_heads = lambda a, n_stream, n_seq: a.reshape(n_stream, n_seq, N_A_HEADS, HEAD_DIM)
    return (yp.reshape(bsz, seq, D_MODEL), ys.reshape(dbsz, dseq, D_MODEL),
            to_heads(kp, bsz, seq), to_heads(vp, bsz, seq), to_heads(ks, dbsz, dseq), to_heads(vs, dbsz, dseq),
            zs.reshape(dbsz, dseq, N_B_GROUPS, B_GROUP_DIM))


def _prepare(ln_mix_g, w_in, sgu_ln_g, sgu_ln_b, w_spatial, b_spatial, out_norm_a_g, out_norm_b_g, w_out,
             ln_ffn_g, w_router, b_router, w_gate_up, b_gate_up, w_down, b_down, final_norm_g):
    depth = w_in.shape[0]
    row = lambda a: a.reshape(depth, 1, -1)
    blk = jnp.arange(MLP_CHUNK, dtype=jnp.int32) // CHUNK
    mask = blk[None, :] <= blk[:, None]
    grp = jnp.arange(B_WIDTH, dtype=jnp.int32) // B_GROUP_DIM
    gavg = jnp.where(grp[:, None] == grp[None, :], 1.0 / B_GROUP_DIM, 0.0).astype(BF16)
    wr = jnp.pad(w_router, ((0, 0), (0, 0), (0, ROUTER_LANES - N_EXPERTS)))
    wr_hi = wr.astype(BF16)
    wr_lo = (wr - wr_hi.astype(F32)).astype(BF16)
    return dict(
        depth=depth,
        ln_mix_g=row(ln_mix_g), w_in=w_in.astype(BF16), sgu_ln_g=row(sgu_ln_g), sgu_ln_b=row(sgu_ln_b),
        gavg=gavg,
        w_spatial=jnp.where(mask[None, None], w_spatial, 0.0).astype(BF16),
        b_spatial=jnp.repeat(jnp.swapaxes(b_spatial, 1, 2), B_GROUP_DIM, axis=2),
        out_norm_a_g=row(out_norm_a_g), out_norm_b_g=row(out_norm_b_g), w_out=w_out.astype(BF16),
        ln_ffn_g=row(ln_ffn_g), wr_hi=wr_hi, wr_lo=wr_lo,
        b_router=row(jnp.pad(b_router, ((0, 0), (0, ROUTER_LANES - N_EXPERTS)))),
        w_gate_up=w_gate_up, b_gate_up=b_gate_up[:, :, None, :],
        w_down=w_down, b_down=b_down[:, :, None, :],
        final_norm_g=final_norm_g.reshape(1, -1),
    )


def kernel(x_prompt, x_sample, cache_k, cache_v, ln_mix_g, w_in, sgu_ln_g, sgu_ln_b, w_spatial, b_spatial,
           out_norm_a_g, out_norm_b_g, w_out, ln_ffn_g, w_router, b_router, w_gate_up, b_gate_up, w_down,
           b_down, final_norm_g):
    w = _prepare(ln_mix_g, w_in, sgu_ln_g, sgu_ln_b, w_spatial, b_spatial, out_norm_a_g, out_norm_b_g, w_out,
                 ln_ffn_g, w_router, b_router, w_gate_up, b_gate_up, w_down, b_down, final_norm_g)
    depth, dbsz, past = cache_k.shape[:3]
    ck = cache_k.reshape(depth, dbsz, past, A_WIDTH)
    cv = cache_v.reshape(depth, dbsz, past, A_WIDTH)
    xp, xs = x_prompt, x_sample
    kps, vps, kss, vss, zss = [], [], [], [], []
    for l in range(depth):
        xp, xs, kp, vp, ks, vs, zs = _layer(l, xp, xs, ck, cv, w)
        kps.append(kp)
        vps.append(vp)
        kss.append(ks)
        vss.append(vs)
        zss.append(zs)
    return (xp, xs, jnp.stack(kps), jnp.stack(vps), jnp.stack(kss), jnp.stack(vss), jnp.stack(zss))
```

```python
import functools
import math

import jax
import jax.numpy as jnp
from jax import lax
from jax.experimental import pallas as pl
from jax.experimental.pallas import tpu as pltpu

F32 = jnp.float32
BF16 = jnp.bfloat16

D_MODEL = 1024
HEAD_DIM = 64
N_A_HEADS = 8
A_WIDTH = N_A_HEADS * HEAD_DIM
N_B_GROUPS = 8
B_GROUP_DIM = 64
B_WIDTH = N_B_GROUPS * B_GROUP_DIM
CHUNK = 64
MLP_CHUNK = 128
SB_SCALE = HEAD_DIM ** -0.5
N_EXPERTS = 32
TOP_K = 4
D_FF = D_MODEL
SWIGLU_LIMIT = 7.0
SWIGLU_ALPHA = 1.702
NORM_EPS = 1e-5

LANES = 128
HEADS_PER_TILE = LANES // HEAD_DIM
N_HEAD_TILES = A_WIDTH // LANES
ROUTER_LANES = 128
VMEM_LIMIT = 56 * 1024 * 1024
TOKEN_TILE = 256
ATTN_Q_TILE = 256
ATTN_K_TILE = 256
MOE_BLOCK = 512
WEIGHT_CAST_ROWS = 128
FF_CHUNK = 256
ZERO_WEIGHT_LOG = -110.0


def _params(sem):
    return pltpu.CompilerParams(dimension_semantics=sem, vmem_limit_bytes=VMEM_LIMIT)


def _split_bf16(t):
    hi = t.astype(BF16)
    lo = (t - hi.astype(F32)).astype(BF16)
    return hi, lo


def _dot(a, b):
    return jnp.dot(a, b, preferred_element_type=F32)


def _dot_nt(a, b):
    return lax.dot_general(a, b, (((1,), (1,)), ((), ())), preferred_element_type=F32)


def _rms(x, g):
    return x * lax.rsqrt(jnp.mean(x * x, axis=-1, keepdims=True) + NORM_EPS) * g


def _inproj_kernel(x_ref, g_ref, w_ref, lng_ref, lnb_ref, gavg_ref,
                   q_ref, k_ref, v_ref, kb_ref, vb_ref, u_ref, zn_ref, znb_ref):
    hn = _rms(x_ref[...], g_ref[...]).astype(BF16)

    def proj(sec):
        return _dot(hn, w_ref[:, sec * A_WIDTH:(sec + 1) * A_WIDTH])

    q_ref[...] = (proj(0) * SB_SCALE).astype(BF16)
    k = proj(1)
    k_ref[...] = k
    kb_ref[...] = k.astype(BF16)
    v = proj(2)
    v_ref[...] = v
    vb_ref[...] = v.astype(BF16)
    u_ref[...] = jax.nn.gelu(proj(3)).astype(BF16)
    z = jax.nn.gelu(proj(4))
    gavg = gavg_ref[...]

    def group_mean(t):
        hi, lo = _split_bf16(t)
        return _dot(hi, gavg) + _dot(lo, gavg)

    zc = z - group_mean(z)
    var = group_mean(zc * zc)
    zn = zc * lax.rsqrt(var + NORM_EPS) * lng_ref[...] + lnb_ref[...]
    zn_ref[...] = zn
    znb_ref[...] = zn.astype(BF16)


def _inproj(x, g, w_bf, lng, lnb, gavg):
    t = x.shape[0]
    tm = min(TOKEN_TILE, t)
    row = lambda w: pl.BlockSpec((tm, w), lambda i: (i, 0))
    full = lambda a: pl.BlockSpec(a.shape, lambda i: (0,) * a.ndim)
    sds = lambda w, dt: jax.ShapeDtypeStruct((t, w), dt)
    return pl.pallas_call(
        _inproj_kernel,
        grid=(t // tm,),
        in_specs=[row(D_MODEL), full(g), full(w_bf), full(lng), full(lnb), full(gavg)],
        out_specs=[row(A_WIDTH)] * 8,
        out_shape=[sds(A_WIDTH, BF16), sds(A_WIDTH, F32), sds(A_WIDTH, F32), sds(A_WIDTH, BF16),
                   sds(A_WIDTH, BF16), sds(B_WIDTH, BF16), sds(B_WIDTH, F32), sds(B_WIDTH, BF16)],
        compiler_params=_params(("parallel",)),
        name="inproj",
    )(x, g, w_bf, lng, lnb, gavg)


def _log_terms(z):
    soft = jnp.log(1.0 + jnp.exp(-jnp.abs(z)))
    return jnp.minimum(z, 0.0) - soft, jnp.minimum(-z, 0.0) - soft


def _tri(n):
    j = lax.broadcasted_iota(jnp.int32, (n, n), 0)
    s = lax.broadcasted_iota(jnp.int32, (n, n), 1)
    return jnp.where(j > s, 1.0, 0.0).astype(BF16)


def _later_sum(log_keep, tri):
    hi, lo = _split_bf16(log_keep)
    return _dot(hi, tri) + _dot(lo, tri)


def _attn_kernel(q_ref, kd_ref, vd_ref, kp_ref, vp_ref, o_ref, *, n_past_fn, tk):
    tq = q_ref.shape[0]
    q = q_ref[...].astype(F32)
    lane = lax.broadcasted_iota(jnp.int32, (tq, LANES), 1)
    qs = [jnp.where((lane // HEAD_DIM) == h, q, 0.0).astype(BF16) for h in range(HEADS_PER_TILE)]

    kd = kd_ref[...].astype(BF16)
    vd = vd_ref[...].astype(BF16)
    t_pos = lax.broadcasted_iota(jnp.int32, (tq, tq), 0)
    s_pos = lax.broadcasted_iota(jnp.int32, (tq, tq), 1)
    visible = s_pos < t_pos
    tri_d = _tri(tq)
    accs, carries = [], []
    for qh in qs:
        log_beta, log_keep = _log_terms(_dot_nt(qh, kd))
        log_keep = jnp.where(visible, log_keep, 0.0)
        a = jnp.where(visible, jnp.exp(log_beta + _later_sum(log_keep, tri_d)), 0.0)
        accs.append(_dot(a.astype(BF16), vd))
        carries.append(jnp.sum(log_keep, axis=-1, keepdims=True))

    n_past = n_past_fn()
    tri_p = _tri(tk)

    def carry_max(carries):
        return functools.reduce(jnp.maximum, [jnp.max(c) for c in carries])

    def cond(state):
        j, _, _, cmax = state
        return jnp.logical_and(j < n_past, cmax > ZERO_WEIGHT_LOG)

    def body(state):
        j, accs, carries, _ = state
        start = pl.multiple_of((n_past - 1 - j) * tk, tk)
        kp = kp_ref[pl.ds(start, tk), :].astype(BF16)
        vp = vp_ref[pl.ds(start, tk), :].astype(BF16)
        new_accs, new_carries = [], []
        for qh, acc, carry in zip(qs, accs, carries):
            log_beta, log_keep = _log_terms(_dot_nt(qh, kp))
            a = jnp.exp(log_beta + _later_sum(log_keep, tri_p) + carry)
            new_accs.append(acc + _dot(a.astype(BF16), vp))
            new_carries.append(carry + jnp.sum(log_keep, axis=-1, keepdims=True))
        return j + 1, tuple(new_accs), tuple(new_carries), carry_max(new_carries)

    _, accs, _, _ = lax.while_loop(cond, body, (jnp.int32(0), tuple(accs), tuple(carries), carry_max(carries)))
    out = accs[0]
    for h in range(1, HEADS_PER_TILE):
        out = jnp.where((lane // HEAD_DIM) == h, accs[h], out)
    o_ref[...] = out


def _attn_prompt(q_bf, k_bf, v_bf):
    b, s, _ = q_bf.shape
    tq = min(ATTN_Q_TILE, s)
    tk = tq
    blk = pl.BlockSpec((None, tq, LANES), lambda bi, hp, i: (bi, i, hp))
    seq = pl.BlockSpec((None, s, LANES), lambda bi, hp, i: (bi, 0, hp))
    kern = functools.partial(_attn_kernel, n_past_fn=lambda: pl.program_id(2) * (tq // tk), tk=tk)
    return pl.pallas_call(
        kern,
        grid=(b, N_HEAD_TILES, s // tq),
        in_specs=[blk, blk, blk, seq, seq],
        out_specs=blk,
        out_shape=jax.ShapeDtypeStruct((b, s, A_WIDTH), F32),
        compiler_params=_params(("parallel", "parallel", "arbitrary")),
        name="attn_prompt",
    )(q_bf, k_bf, v_bf, k_bf, v_bf)


def _attn_sample(q_bf, k_bf, v_bf, cache_k, cache_v, layer):
    b, n, _ = q_bf.shape
    p = cache_k.shape[2]
    tk = min(ATTN_K_TILE, p)
    blk = pl.BlockSpec((None, n, LANES), lambda bi, hp: (bi, 0, hp))
    past = pl.BlockSpec((None, None, p, LANES), lambda bi, hp: (layer, bi, 0, hp))
    kern = functools.partial(_attn_kernel, n_past_fn=lambda: p // tk, tk=tk)
    return pl.pallas_call(
        kern,
        grid=(b, N_HEAD_TILES),
        in_specs=[blk, blk, blk, past, past],
        out_specs=blk,
        out_shape=jax.ShapeDtypeStruct((b, n, A_WIDTH), F32),
        compiler_params=_params(("parallel", "parallel")),
        name="attn_sample",
    )(q_bf, k_bf, v_bf, cache_k, cache_v)


def _mixout_kernel(x_ref, oa_ref, u_ref, znb_ref, wsp_ref, bsp_ref, ga_ref, gb_ref, wout_ref,
                   lnf_ref, wrh_ref, wrl_ref, br_ref, x1_ref, h2_ref, lg_ref, mix_ref):
    tm = x_ref.shape[0]
    chunk = wsp_ref.shape[1]
    group = lax.broadcasted_iota(jnp.int32, (chunk, B_WIDTH), 1) // B_GROUP_DIM
    for c in range(tm // chunk):
        zc = znb_ref[c * chunk:(c + 1) * chunk, :]
        m = jnp.zeros((chunk, B_WIDTH), F32)
        for g in range(N_B_GROUPS):
            m = jnp.where(group == g, _dot(wsp_ref[g], zc), m)
        mix_ref[c * chunk:(c + 1) * chunk, :] = m + bsp_ref[...]
    ob = u_ref[...].astype(F32) * mix_ref[...]
    ya = _rms(oa_ref[...], ga_ref[...]).astype(BF16)
    yb = _rms(ob, gb_ref[...]).astype(BF16)
    x1 = x_ref[...] + _dot(ya, wout_ref[:A_WIDTH, :]) + _dot(yb, wout_ref[A_WIDTH:, :])
    x1_ref[...] = x1
    h2 = _rms(x1, lnf_ref[...])
    hi, lo = _split_bf16(h2)
    h2_ref[...] = hi
    wrh = wrh_ref[...]
    lg_ref[...] = _dot(hi, wrh) + _dot(lo, wrh) + _dot(hi, wrl_ref[...]) + br_ref[...]


def _mixout(x, oa, u, znb, wsp, bsp, ga, gb, wout_bf, lnf, wrh, wrl, br):
    t = x.shape[0]
    tm = min(TOKEN_TILE, t)
    row = lambda w: pl.BlockSpec((tm, w), lambda i: (i, 0))
    full = lambda a: pl.BlockSpec(a.shape, lambda i: (0,) * a.ndim)
    return pl.pallas_call(
        _mixout_kernel,
        grid=(t // tm,),
        in_specs=[row(D_MODEL), row(A_WIDTH), row(B_WIDTH), row(B_WIDTH), full(wsp), full(bsp),
                  full(ga), full(gb), full(wout_bf), full(lnf), full(wrh), full(wrl), full(br)],
        out_specs=[row(D_MODEL), row(D_MODEL), row(ROUTER_LANES)],
        out_shape=[jax.ShapeDtypeStruct((t, D_MODEL), F32), jax.ShapeDtypeStruct((t, D_MODEL), BF16),
                   jax.ShapeDtypeStruct((t, ROUTER_LANES), F32)],
        scratch_shapes=[pltpu.VMEM((tm, B_WIDTH), F32)],
        compiler_params=_params(("parallel",)),
        name="mixout",
    )(x, oa, u, znb, wsp, bsp, ga, gb, wout_bf, lnf, wrh, wrl, br)


def _moe_kernel(bexp_ref, bval_ref, xs_ref, wgu_ref, bgu_ref, wd_ref, bd_ref, o_ref, wgu_bf, wd_bf):
    i = pl.program_id(0)
    valid = bval_ref[i] > 0
    new_expert = jnp.logical_or(i == 0, bexp_ref[i] != bexp_ref[jnp.maximum(i - 1, 0)])

    @pl.when(jnp.logical_and(valid, new_expert))
    def _():
        def cast_rows(r, _):
            rows = pl.ds(pl.multiple_of(r * WEIGHT_CAST_ROWS, WEIGHT_CAST_ROWS), WEIGHT_CAST_ROWS)
            wgu_bf[rows, :] = wgu_ref[rows, :].astype(BF16)
            wd_bf[rows, :] = wd_ref[rows, :].astype(BF16)
            return 0
        lax.fori_loop(0, D_MODEL // WEIGHT_CAST_ROWS, cast_rows, 0)

    @pl.when(valid)
    def _():
        xs = xs_ref[...]
        acc = None
        for c in range(D_FF // FF_CHUNK):
            lo, hi = c * FF_CHUNK, (c + 1) * FF_CHUNK
            gate = _dot(xs, wgu_bf[:, lo:hi]) + bgu_ref[:, lo:hi]
            lin = _dot(xs, wgu_bf[:, D_FF + lo:D_FF + hi]) + bgu_ref[:, D_FF + lo:D_FF + hi]
            gate = jnp.minimum(gate, SWIGLU_LIMIT)
            lin = jnp.clip(lin, -SWIGLU_LIMIT, SWIGLU_LIMIT)
            act = gate * jax.nn.sigmoid(SWIGLU_ALPHA * gate) * (lin + 1.0)
            part = _dot(act.astype(BF16), wd_bf[lo:hi, :])
            acc = part if acc is None else acc + part
        o_ref[...] = (acc + bd_ref[...]).astype(o_ref.dtype)

    @pl.when(bval_ref[i] == 0)
    def _():
        o_ref[...] = jnp.zeros_like(o_ref)


def _moe(block_exp, block_valid, xs, wgu, bgu, wd, bd, layer):
    n_slots = xs.shape[0]
    bm = MOE_BLOCK
    grid_spec = pltpu.PrefetchScalarGridSpec(
        num_scalar_prefetch=2,
        grid=(n_slots // bm,),
        in_specs=[
            pl.BlockSpec((bm, D_MODEL), lambda i, be, bv: (i, 0)),
            pl.BlockSpec((None, None, D_MODEL, 2 * D_FF), lambda i, be, bv: (layer, be[i], 0, 0)),
            pl.BlockSpec((None, None, 1, 2 * D_FF), lambda i, be, bv: (layer, be[i], 0, 0)),
            pl.BlockSpec((None, None, D_FF, D_MODEL), lambda i, be, bv: (layer, be[i], 0, 0)),
            pl.BlockSpec((None, None, 1, D_MODEL), lambda i, be, bv: (layer, be[i], 0, 0)),
        ],
        out_specs=pl.BlockSpec((bm, D_MODEL), lambda i, be, bv: (i, 0)),
        scratch_shapes=[pltpu.VMEM((D_MODEL, 2 * D_FF), BF16), pltpu.VMEM((D_FF, D_MODEL), BF16)],
    )
    return pl.pallas_call(
        _moe_kernel,
        grid_spec=grid_spec,
        out_shape=jax.ShapeDtypeStruct((n_slots, D_MODEL), BF16),
        compiler_params=_params(("arbitrary",)),
        name="moe",
    )(block_exp, block_valid, xs, wgu, bgu, wd, bd)


def _combine_kernel(x_ref, rows_ref, gates_ref, g_ref, y_ref, *, final):
    y = x_ref[...]
    gates = gates_ref[...]
    for k in range(TOP_K):
        y = y + gates[:, k:k + 1] * rows_ref[k].astype(F32)
    y_ref[...] = _rms(y, g_ref[...]) if final else y


def _combine(x1, rows, gates, g, final, row_offset):
    t = x1.shape[0]
    tm = min(TOKEN_TILE, t)
    off = row_offset // tm
    return pl.pallas_call(
        functools.partial(_combine_kernel, final=final),
        grid=(t // tm,),
        in_specs=[pl.BlockSpec((tm, D_MODEL), lambda i: (i, 0)),
                  pl.BlockSpec((TOP_K, tm, D_MODEL), lambda i: (0, i + off, 0)),
                  pl.BlockSpec((tm, ROUTER_LANES), lambda i: (i + off, 0)),
                  pl.BlockSpec(g.shape, lambda i: (0, 0))],
        out_specs=pl.BlockSpec((tm, D_MODEL), lambda i: (i, 0)),
        out_shape=jax.ShapeDtypeStruct((t, D_MODEL), F32),
        compiler_params=_params(("parallel",)),
        name="combine",
    )(x1, rows, gates, g)


def _route_kernel(lg_ref, slots_ref, gates_ref, counts_ref, base_ref, pstart_ref):
    phase = pl.program_id(0)
    i = pl.program_id(1)
    tm = lg_ref.shape[0]
    lane = lax.broadcasted_iota(jnp.int32, (tm, ROUTER_LANES), 1)
    lane_f = lane.astype(F32)
    x = jnp.where(lane < N_EXPERTS, lg_ref[...], -jnp.inf)
    vals, idxs = [], []
    sel = jnp.zeros((tm, ROUTER_LANES), F32)
    for _ in range(TOP_K):
        m = jnp.max(x, axis=-1, keepdims=True)
        idx = jnp.min(jnp.where(x == m, lane_f, float(ROUTER_LANES)), axis=-1, keepdims=True)
        hit = lane_f == idx
        x = jnp.where(hit, -jnp.inf, x)
        sel = jnp.where(hit, 1.0, sel)
        vals.append(m)
        idxs.append(idx)
    tile_counts = jnp.sum(sel, axis=0, keepdims=True)

    @pl.when(jnp.logical_and(phase == 0, i == 0))
    def _():
        base_ref[...] = jnp.zeros_like(base_ref)

    @pl.when(phase == 0)
    def _():
        base_ref[...] = base_ref[...] + tile_counts

    @pl.when(jnp.logical_and(phase == 1, i == 0))
    def _():
        counts = base_ref[...]
        counts_ref[...] = counts
        padded = jnp.ceil(counts * (1.0 / MOE_BLOCK)) * MOE_BLOCK
        lane8 = lax.broadcasted_iota(jnp.int32, padded.shape, 1)
        pend = padded
        shift = 1
        while shift < ROUTER_LANES:
            pend = pend + jnp.where(lane8 >= shift, pltpu.roll(pend, shift, axis=1), 0.0)
            shift *= 2
        pstart_ref[...] = pend - padded
        base_ref[...] = jnp.zeros_like(base_ref)

    @pl.when(phase == 1)
    def _():
        r = lax.broadcasted_iota(jnp.int32, (tm, tm), 0)
        c = lax.broadcasted_iota(jnp.int32, (tm, tm), 1)
        earlier = jnp.where(c < r, 1.0, 0.0).astype(BF16)
        pos = _dot(earlier, sel.astype(BF16)) + base_ref[0:1, :] + pstart_ref[0:1, :]
        weights = [jnp.exp(v - vals[0]) for v in vals]
        total = functools.reduce(lambda a, b: a + b, weights)
        slots = jnp.zeros((tm, ROUTER_LANES), jnp.int32)
        gates = jnp.zeros((tm, ROUTER_LANES), F32)
        for k in range(TOP_K):
            slot_k = jnp.sum(jnp.where(lane_f == idxs[k], pos, 0.0), axis=-1, keepdims=True)
            slots = jnp.where(lane == k, slot_k.astype(jnp.int32), slots)
            gates = jnp.where(lane == k, weights[k] / total, gates)
        slots_ref[...] = slots
        gates_ref[...] = gates
        base_ref[...] = base_ref[...] + tile_counts


def _route(logits):
    t = logits.shape[0]
    tm = math.gcd(TOKEN_TILE, t)
    out_blk = pl.BlockSpec((tm, ROUTER_LANES), lambda p, i: (i * p, 0))
    sub = 8
    return pl.pallas_call(
        _route_kernel,
        grid=(2, t // tm),
        in_specs=[pl.BlockSpec((tm, ROUTER_LANES), lambda p, i: (i, 0))],
        out_specs=[out_blk, out_blk, pl.BlockSpec((sub, ROUTER_LANES), lambda p, i: (0, 0))],
        out_shape=[jax.ShapeDtypeStruct((t, ROUTER_LANES), jnp.int32),
                   jax.ShapeDtypeStruct((t, ROUTER_LANES), F32),
                   jax.ShapeDtypeStruct((sub, ROUTER_LANES), F32)],
        scratch_shapes=[pltpu.VMEM((sub, ROUTER_LANES), F32), pltpu.VMEM((sub, ROUTER_LANES), F32)],
        compiler_params=_params(("arbitrary", "arbitrary")),
        name="route",
    )(logits)


def _block_tables(counts, n_blocks):
    padded = (counts + MOE_BLOCK - 1) // MOE_BLOCK * MOE_BLOCK
    pend = jnp.cumsum(padded)
    bstart = jnp.arange(n_blocks, dtype=jnp.int32) * MOE_BLOCK
    block_exp = jnp.minimum(jnp.sum((pend[None, :] <= bstart[:, None]).astype(jnp.int32), axis=1), N_EXPERTS - 1)
    block_valid = (bstart < pend[-1]).astype(jnp.int32)
    return block_exp, block_valid


def _layer(l, xp, xs, cache_k, cache_v, w):
    bsz, seq, _ = xp.shape
    dbsz, dseq, _ = xs.shape
    tp, ts = bsz * seq, dbsz * dseq

    def dense(x2, n_stream, n_seq, chunk, attn_fn):
        q, k, v, kb, vb, u, zn, znb = _inproj(x2, w['ln_mix_g'][l], w['w_in'][l], w['sgu_ln_g'][l],
                                              w['sgu_ln_b'][l], w['gavg'])
        r3 = lambda a: a.reshape(n_stream, n_seq, A_WIDTH)
        oa = attn_fn(r3(q), r3(kb), r3(vb)).reshape(n_stream * n_seq, A_WIDTH)
        wsp = w['w_spatial'][l][:, :chunk, :chunk]
        bsp = w['b_spatial'][l][:chunk]
        x1, h2, lg = _mixout(x2, oa, u, znb, wsp, bsp, w['out_norm_a_g'][l], w['out_norm_b_g'][l],
                             w['w_out'][l], w['ln_ffn_g'][l], w['wr_hi'][l], w['wr_lo'][l], w['b_router'][l])
        return x1, h2, lg, k, v, zn

    x1p, h2p, lgp, kp, vp, _ = dense(xp.reshape(tp, D_MODEL), bsz, seq, MLP_CHUNK, _attn_prompt)
    x1s, h2s, lgs, ks, vs, zs = dense(xs.reshape(ts, D_MODEL), dbsz, dseq, dseq,
                                      lambda q, k, v: _attn_sample(q, k, v, cache_k, cache_v, l))

    t = tp + ts
    h2 = jnp.concatenate([h2p, h2s, jnp.zeros((1, D_MODEL), BF16)], axis=0)
    slots, gates, counts = _route(jnp.concatenate([lgp, lgs], axis=0))
    tok_slot = slots[:, :TOP_K]
    n_rows = t * TOP_K
    n_blocks = -(-n_rows // MOE_BLOCK) + N_EXPERTS
    block_exp, block_valid = _block_tables(counts[0, :N_EXPERTS].astype(jnp.int32), n_blocks)
    flat_tok = jnp.arange(n_rows, dtype=jnp.int32) // TOP_K
    slot_tok = jnp.full((n_blocks * MOE_BLOCK,), t, jnp.int32).at[tok_slot.reshape(-1)].set(flat_tok)
    x_sorted = jnp.take(h2, slot_tok, axis=0)
    out = _moe(block_exp, block_valid, x_sorted, w['w_gate_up'], w['b_gate_up'], w['w_down'], w['b_down'], l)
    rows = jnp.take(out, tok_slot.T, axis=0)
    final = l == w['depth'] - 1
    yp = _combine(x1p, rows, gates, w['final_norm_g'], final, 0)
    ys = _combine(x1s, rows, gates, w['final_norm_g'], final, tp)
    to_heads = lambda a, n_stream, n_seq: a.reshape(n_stream, n_seq, N_A_HEADS, HEAD_DIM)
    return (yp.reshape(bsz, seq, D_MODEL), ys.reshape(dbsz, dseq, D_MODEL),
            to_heads(kp, bsz, seq), to_heads(vp, bsz, seq), to_heads(ks, dbsz, dseq), to_heads(vs, dbsz, dseq),
            zs.reshape(dbsz, dseq, N_B_GROUPS, B_GROUP_DIM))


def _prepare(ln_mix_g, w_in, sgu_ln_g, sgu_ln_b, w_spatial, b_spatial, out_norm_a_g, out_norm_b_g, w_out,
             ln_ffn_g, w_router, b_router, w_gate_up, b_gate_up, w_down, b_down, final_norm_g):
    depth = w_in.shape[0]
    row = lambda a: a.reshape(depth, 1, -1)
    blk = jnp.arange(MLP_CHUNK, dtype=jnp.int32) // CHUNK
    mask = blk[None, :] <= blk[:, None]
    grp = jnp.arange(B_WIDTH, dtype=jnp.int32) // B_GROUP_DIM
    gavg = jnp.where(grp[:, None] == grp[None, :], 1.0 / B_GROUP_DIM, 0.0).astype(BF16)
    wr = jnp.pad(w_router, ((0, 0), (0, 0), (0, ROUTER_LANES - N_EXPERTS)))
    wr_hi = wr.astype(BF16)
    wr_lo = (wr - wr_hi.astype(F32)).astype(BF16)
    return dict(
        depth=depth,
        ln_mix_g=row(ln_mix_g), w_in=w_in.astype(BF16), sgu_ln_g=row(sgu_ln_g), sgu_ln_b=row(sgu_ln_b),
        gavg=gavg,
        w_spatial=jnp.where(mask[None, None], w_spatial, 0.0).astype(BF16),
        b_spatial=jnp.repeat(jnp.swapaxes(b_spatial, 1, 2), B_GROUP_DIM, axis=2),
        out_norm_a_g=row(out_norm_a_g), out_norm_b_g=row(out_norm_b_g), w_out=w_out.astype(BF16),
        ln_ffn_g=row(ln_ffn_g), wr_hi=wr_hi, wr_lo=wr_lo,
        b_router=row(jnp.pad(b_router, ((0, 0), (0, ROUTER_LANES - N_EXPERTS)))),
        w_gate_up=w_gate_up, b_gate_up=b_gate_up[:, :, None, :],
        w_down=w_down, b_down=b_down[:, :, None, :],
        final_norm_g=final_norm_g.reshape(1, -1),
    )


def kernel(x_prompt, x_sample, cache_k, cache_v, ln_mix_g, w_in, sgu_ln_g, sgu_ln_b, w_spatial, b_spatial,
           out_norm_a_g, out_norm_b_g, w_out, ln_ffn_g, w_router, b_router, w_gate_up, b_gate_up, w_down,
           b_down, final_norm_g):
    w = _prepare(ln_mix_g, w_in, sgu_ln_g, sgu_ln_b, w_spatial, b_spatial, out_norm_a_g, out_norm_b_g, w_out,
                 ln_ffn_g, w_router, b_router, w_gate_up, b_gate_up, w_down, b_down, final_norm_g)
    depth, dbsz, past = cache_k.shape[:3]
    ck = cache_k.reshape(depth, dbsz, past, A_WIDTH)
    cv = cache_v.reshape(depth, dbsz, past, A_WIDTH)
    xp, xs = x_prompt, x_sample
    kps, vps, kss, vss, zss = [], [], [], [], []
    for l in range(depth):
        xp, xs, kp, vp, ks, vs, zs = _layer(l, xp, xs, ck, cv, w)
        kps.append(kp)
        vps.append(vp)
        kss.append(ks)
        vss.append(vs)
        zss.append(zs)
    return (xp, xs, jnp.stack(kps), jnp.stack(vps), jnp.stack(kss), jnp.stack(vss), jnp.stack(zss))
```

```python
import functools
import math

import jax
import jax.numpy as jnp
from jax import lax
from jax.experimental import pallas as pl
from jax.experimental.pallas import tpu as pltpu

F32 = jnp.float32
BF16 = jnp.bfloat16

D_MODEL = 1024
HEAD_DIM = 64
N_A_HEADS = 8
A_WIDTH = N_A_HEADS * HEAD_DIM
N_B_GROUPS = 8
B_GROUP_DIM = 64
B_WIDTH = N_B_GROUPS * B_GROUP_DIM
CHUNK = 64
MLP_CHUNK = 128
SB_SCALE = HEAD_DIM ** -0.5
N_EXPERTS = 32
TOP_K = 4
D_FF = D_MODEL
SWIGLU_LIMIT = 7.0
SWIGLU_ALPHA = 1.702
NORM_EPS = 1e-5

LANES = 128
HEADS_PER_TILE = LANES // HEAD_DIM
N_HEAD_TILES = A_WIDTH // LANES
ROUTER_LANES = 128
VMEM_LIMIT = 56 * 1024 * 1024
TOKEN_TILE = 256
ATTN_Q_TILE = 256
ATTN_K_TILE = 256
MOE_BLOCK = 512
WEIGHT_CAST_ROWS = 128
FF_CHUNK = 256
RUN_ROWS = 16
ZERO_WEIGHT_LOG = -110.0


def _params(sem):
    return pltpu.CompilerParams(dimension_semantics=sem, vmem_limit_bytes=VMEM_LIMIT)


def _split_bf16(t):
    hi = t.astype(BF16)
    lo = (t - hi.astype(F32)).astype(BF16)
    return hi, lo


def _dot(a, b):
    return jnp.dot(a, b, preferred_element_type=F32)


def _dot_nt(a, b):
    return lax.dot_general(a, b, (((1,), (1,)), ((), ())), preferred_element_type=F32)


def _rms(x, g):
    return x * lax.rsqrt(jnp.mean(x * x, axis=-1, keepdims=True) + NORM_EPS) * g


def _inproj_kernel(x_ref, g_ref, w_ref, lng_ref, lnb_ref, gavg_ref,
                   q_ref, k_ref, v_ref, kb_ref, vb_ref, u_ref, zn_ref, znb_ref):
    hn = _rms(x_ref[...], g_ref[...]).astype(BF16)

    def proj(sec):
        return _dot(hn, w_ref[:, sec * A_WIDTH:(sec + 1) * A_WIDTH])

    q_ref[...] = (proj(0) * SB_SCALE).astype(BF16)
    k = proj(1)
    k_ref[...] = k
    kb_ref[...] = k.astype(BF16)
    v = proj(2)
    v_ref[...] = v
    vb_ref[...] = v.astype(BF16)
    u_ref[...] = jax.nn.gelu(proj(3)).astype(BF16)
    z = jax.nn.gelu(proj(4))
    gavg = gavg_ref[...]

    def group_mean(t):
        hi, lo = _split_bf16(t)
        return _dot(hi, gavg) + _dot(lo, gavg)

    zc = z - group_mean(z)
    var = group_mean(zc * zc)
    zn = zc * lax.rsqrt(var + NORM_EPS) * lng_ref[...] + lnb_ref[...]
    zn_ref[...] = zn
    znb_ref[...] = zn.astype(BF16)


def _inproj(x, g, w_bf, lng, lnb, gavg):
    t = x.shape[0]
    tm = min(TOKEN_TILE, t)
    row = lambda w: pl.BlockSpec((tm, w), lambda i: (i, 0))
    full = lambda a: pl.BlockSpec(a.shape, lambda i: (0,) * a.ndim)
    sds = lambda w, dt: jax.ShapeDtypeStruct((t, w), dt)
    return pl.pallas_call(
        _inproj_kernel,
        grid=(t // tm,),
        in_specs=[row(D_MODEL), full(g), full(w_bf), full(lng), full(lnb), full(gavg)],
        out_specs=[row(A_WIDTH)] * 8,
        out_shape=[sds(A_WIDTH, BF16), sds(A_WIDTH, F32), sds(A_WIDTH, F32), sds(A_WIDTH, BF16),
                   sds(A_WIDTH, BF16), sds(B_WIDTH, BF16), sds(B_WIDTH, F32), sds(B_WIDTH, BF16)],
        compiler_params=_params(("parallel",)),
        name="inproj",
    )(x, g, w_bf, lng, lnb, gavg)


def _log_terms(z):
    soft = jnp.log(1.0 + jnp.exp(-jnp.abs(z)))
    return jnp.minimum(z, 0.0) - soft, jnp.minimum(-z, 0.0) - soft


def _tri(n):
    j = lax.broadcasted_iota(jnp.int32, (n, n), 0)
    s = lax.broadcasted_iota(jnp.int32, (n, n), 1)
    return jnp.where(j > s, 1.0, 0.0).astype(BF16)


def _later_sum(log_keep, tri):
    hi, lo = _split_bf16(log_keep)
    return _dot(hi, tri) + _dot(lo, tri)


def _attn_kernel(q_ref, kd_ref, vd_ref, kp_ref, vp_ref, o_ref, *, n_past_fn, tk):
    tq = q_ref.shape[0]
    q = q_ref[...].astype(F32)
    lane = lax.broadcasted_iota(jnp.int32, (tq, LANES), 1)
    qs = [jnp.where((lane // HEAD_DIM) == h, q, 0.0).astype(BF16) for h in range(HEADS_PER_TILE)]

    kd = kd_ref[...].astype(BF16)
    vd = vd_ref[...].astype(BF16)
    t_pos = lax.broadcasted_iota(jnp.int32, (tq, tq), 0)
    s_pos = lax.broadcasted_iota(jnp.int32, (tq, tq), 1)
    visible = s_pos < t_pos
    tri_d = _tri(tq)
    accs, carries = [], []
    for qh in qs:
        log_beta, log_keep = _log_terms(_dot_nt(qh, kd))
        log_keep = jnp.where(visible, log_keep, 0.0)
        a = jnp.where(visible, jnp.exp(log_beta + _later_sum(log_keep, tri_d)), 0.0)
        accs.append(_dot(a.astype(BF16), vd))
        carries.append(jnp.sum(log_keep, axis=-1, keepdims=True))

    n_past = n_past_fn()
    tri_p = _tri(tk)

    def carry_max(carries):
        return functools.reduce(jnp.maximum, [jnp.max(c) for c in carries])

    def cond(state):
        j, _, _, cmax = state
        return jnp.logical_and(j < n_past, cmax > ZERO_WEIGHT_LOG)

    def past_block(j, accs, carries, live=None):
        start = pl.multiple_of(jnp.maximum(n_past - 1 - j, 0) * tk, tk)
        kp = kp_ref[pl.ds(start, tk), :].astype(BF16)
        vp = vp_ref[pl.ds(start, tk), :].astype(BF16)
        new_accs, new_carries = [], []
        for qh, acc, carry in zip(qs, accs, carries):
            log_beta, log_keep = _log_terms(_dot_nt(qh, kp))
            a = jnp.exp(log_beta + _later_sum(log_keep, tri_p) + carry)
            av = _dot(a.astype(BF16), vp)
            new_accs.append(acc + (av if live is None else jnp.where(live, av, 0.0)))
            new_carries.append(carry + jnp.sum(log_keep, axis=-1, keepdims=True))
        return tuple(new_accs), tuple(new_carries)

    def body(state):
        j, accs, carries, _ = state
        accs, carries = past_block(j, accs, carries)
        return j + 1, accs, carries, carry_max(carries)

    accs, carries = past_block(0, accs, carries, live=n_past > 0)
    _, accs, _, _ = lax.while_loop(cond, body, (jnp.int32(1), accs, carries, carry_max(carries)))
    out = accs[0]
    for h in range(1, HEADS_PER_TILE):
        out = jnp.where((lane // HEAD_DIM) == h, accs[h], out)
    o_ref[...] = out


def _attn_prompt(q_bf, k_bf, v_bf):
    b, s, _ = q_bf.shape
    tq = min(ATTN_Q_TILE, s)
    tk = tq
    blk = pl.BlockSpec((None, tq, LANES), lambda bi, hp, i: (bi, i, hp))
    seq = pl.BlockSpec((None, s, LANES), lambda bi, hp, i: (bi, 0, hp))
    kern = functools.partial(_attn_kernel, n_past_fn=lambda: pl.program_id(2) * (tq // tk), tk=tk)
    return pl.pallas_call(
        kern,
        grid=(b, N_HEAD_TILES, s // tq),
        in_specs=[blk, blk, blk, seq, seq],
        out_specs=blk,
        out_shape=jax.ShapeDtypeStruct((b, s, A_WIDTH), F32),
        compiler_params=_params(("parallel", "parallel", "arbitrary")),
        name="attn_prompt",
    )(q_bf, k_bf, v_bf, k_bf, v_bf)


def _attn_sample(q_bf, k_bf, v_bf, cache_k, cache_v, layer):
    b, n, _ = q_bf.shape
    p = cache_k.shape[2]
    tk = min(ATTN_K_TILE, p)
    blk = pl.BlockSpec((None, n, LANES), lambda bi, hp: (bi, 0, hp))
    past = pl.BlockSpec((None, None, p, LANES), lambda bi, hp: (layer, bi, 0, hp))
    kern = functools.partial(_attn_kernel, n_past_fn=lambda: p // tk, tk=tk)
    return pl.pallas_call(
        kern,
        grid=(b, N_HEAD_TILES),
        in_specs=[blk, blk, blk, past, past],
        out_specs=blk,
        out_shape=jax.ShapeDtypeStruct((b, n, A_WIDTH), F32),
        compiler_params=_params(("parallel", "parallel")),
        name="attn_sample",
    )(q_bf, k_bf, v_bf, cache_k, cache_v)


def _mixout_kernel(x_ref, oa_ref, u_ref, znb_ref, wsp_ref, bsp_ref, ga_ref, gb_ref, wout_ref,
                   lnf_ref, wrh_ref, wrl_ref, br_ref, x1_ref, h2_ref, lg_ref, mix_ref):
    tm = x_ref.shape[0]
    chunk = wsp_ref.shape[1]
    group = lax.broadcasted_iota(jnp.int32, (chunk, B_WIDTH), 1) // B_GROUP_DIM
    for c in range(tm // chunk):
        zc = znb_ref[c * chunk:(c + 1) * chunk, :]
        m = jnp.zeros((chunk, B_WIDTH), F32)
        for g in range(N_B_GROUPS):
            m = jnp.where(group == g, _dot(wsp_ref[g], zc), m)
        mix_ref[c * chunk:(c + 1) * chunk, :] = m + bsp_ref[...]
    ob = u_ref[...].astype(F32) * mix_ref[...]
    ya = _rms(oa_ref[...], ga_ref[...]).astype(BF16)
    yb = _rms(ob, gb_ref[...]).astype(BF16)
    x1 = x_ref[...] + _dot(ya, wout_ref[:A_WIDTH, :]) + _dot(yb, wout_ref[A_WIDTH:, :])
    x1_ref[...] = x1
    h2 = _rms(x1, lnf_ref[...])
    hi, lo = _split_bf16(h2)
    h2_ref[...] = hi
    wrh = wrh_ref[...]
    lg_ref[...] = _dot(hi, wrh) + _dot(lo, wrh) + _dot(hi, wrl_ref[...]) + br_ref[...]


def _mixout(x, oa, u, znb, wsp, bsp, ga, gb, wout_bf, lnf, wrh, wrl, br):
    t = x.shape[0]
    tm = min(TOKEN_TILE, t)
    row = lambda w: pl.BlockSpec((tm, w), lambda i: (i, 0))
    full = lambda a: pl.BlockSpec(a.shape, lambda i: (0,) * a.ndim)
    return pl.pallas_call(
        _mixout_kernel,
        grid=(t // tm,),
        in_specs=[row(D_MODEL), row(A_WIDTH), row(B_WIDTH), row(B_WIDTH), full(wsp), full(bsp),
                  full(ga), full(gb), full(wout_bf), full(lnf), full(wrh), full(wrl), full(br)],
        out_specs=[row(D_MODEL), row(D_MODEL), row(ROUTER_LANES)],
        out_shape=[jax.ShapeDtypeStruct((t, D_MODEL), F32), jax.ShapeDtypeStruct((t, D_MODEL), BF16),
                   jax.ShapeDtypeStruct((t, ROUTER_LANES), F32)],
        scratch_shapes=[pltpu.VMEM((tm, B_WIDTH), F32)],
        compiler_params=_params(("parallel",)),
        name="mixout",
    )(x, oa, u, znb, wsp, bsp, ga, gb, wout_bf, lnf, wrh, wrl, br)


def _moe_kernel(bexp_ref, bval_ref, xs_ref, wgu_ref, bgu_ref, wd_ref, bd_ref, o_ref, wgu_bf, wd_bf):
    i = pl.program_id(0)
    valid = bval_ref[i] > 0
    new_expert = jnp.logical_or(i == 0, bexp_ref[i] != bexp_ref[jnp.maximum(i - 1, 0)])

    @pl.when(jnp.logical_and(valid, new_expert))
    def _():
        def cast_rows(r, _):
            rows = pl.ds(pl.multiple_of(r * WEIGHT_CAST_ROWS, WEIGHT_CAST_ROWS), WEIGHT_CAST_ROWS)
            wgu_bf[rows, :] = wgu_ref[rows, :].astype(BF16)
            wd_bf[rows, :] = wd_ref[rows, :].astype(BF16)
            return 0
        lax.fori_loop(0, D_MODEL // WEIGHT_CAST_ROWS, cast_rows, 0)

    @pl.when(valid)
    def _():
        xs = xs_ref[...]
        acc = None
        for c in range(D_FF // FF_CHUNK):
            lo, hi = c * FF_CHUNK, (c + 1) * FF_CHUNK
            gate = _dot(xs, wgu_bf[:, lo:hi]) + bgu_ref[:, lo:hi]
            lin = _dot(xs, wgu_bf[:, D_FF + lo:D_FF + hi]) + bgu_ref[:, D_FF + lo:D_FF + hi]
            gate = jnp.minimum(gate, SWIGLU_LIMIT)
            lin = jnp.clip(lin, -SWIGLU_LIMIT, SWIGLU_LIMIT)
            act = gate * jax.nn.sigmoid(SWIGLU_ALPHA * gate) * (lin + 1.0)
            part = _dot(act.astype(BF16), wd_bf[lo:hi, :])
            acc = part if acc is None else acc + part
        o_ref[...] = (acc + bd_ref[...]).astype(o_ref.dtype)

    @pl.when(bval_ref[i] == 0)
    def _():
        o_ref[...] = jnp.zeros_like(o_ref)


def _moe(block_exp, block_valid, xs, wgu, bgu, wd, bd, layer):
    n_slots = xs.shape[0]
    bm = MOE_BLOCK
    grid_spec = pltpu.PrefetchScalarGridSpec(
        num_scalar_prefetch=2,
        grid=(n_slots // bm,),
        in_specs=[
            pl.BlockSpec((bm, D_MODEL), lambda i, be, bv: (i, 0)),
            pl.BlockSpec((None, None, D_MODEL, 2 * D_FF), lambda i, be, bv: (layer, be[i], 0, 0)),
            pl.BlockSpec((None, None, 1, 2 * D_FF), lambda i, be, bv: (layer, be[i], 0, 0)),
            pl.BlockSpec((None, None, D_FF, D_MODEL), lambda i, be, bv: (layer, be[i], 0, 0)),
            pl.BlockSpec((None, None, 1, D_MODEL), lambda i, be, bv: (layer, be[i], 0, 0)),
        ],
        out_specs=pl.BlockSpec((bm, D_MODEL), lambda i, be, bv: (i, 0)),
        scratch_shapes=[pltpu.VMEM((D_MODEL, 2 * D_FF), BF16), pltpu.VMEM((D_FF, D_MODEL), BF16)],
    )
    return pl.pallas_call(
        _moe_kernel,
        grid_spec=grid_spec,
        out_shape=jax.ShapeDtypeStruct((n_slots, D_MODEL), BF16),
        compiler_params=_params(("arbitrary",)),
        name="moe",
    )(block_exp, block_valid, xs, wgu, bgu, wd, bd)


def _collect_kernel(src_ref, cnt_ref, dst_ref, x_ref, pos_ref, gates_ref, g_ref, rows_hbm, y_ref, buf, sem,
                    *, final, tile_off):
    i = pl.program_id(0)
    n = pl.num_programs(0)

    def run_copies(tile, slot, wait):
        def per_expert(e, _):
            idx = (tile + tile_off) * N_EXPERTS + e
            src, dst = src_ref[idx], dst_ref[idx]

            def per_piece(c, _):
                cp = pltpu.make_async_copy(
                    rows_hbm.at[pl.ds(pl.multiple_of((src + c) * RUN_ROWS, RUN_ROWS), RUN_ROWS), :],
                    buf.at[slot, pl.ds(pl.multiple_of((dst + c) * RUN_ROWS, RUN_ROWS), RUN_ROWS), :],
                    sem.at[slot])
                if wait:
                    cp.wait()
                else:
                    cp.start()
                return 0

            lax.fori_loop(0, cnt_ref[idx], per_piece, 0)
            return 0

        lax.fori_loop(0, N_EXPERTS, per_expert, 0)

    @pl.when(i == 0)
    def _():
        buf[...] = jnp.zeros_like(buf)
        run_copies(0, 0, wait=False)

    @pl.when(i + 1 < n)
    def _():
        run_copies(i + 1, (i + 1) % 2, wait=False)

    slot = i % 2
    run_copies(i, slot, wait=True)

    tm = x_ref.shape[0]
    n_buf = buf.shape[1]
    pos = pos_ref[...]
    gates = gates_ref[...]
    r = lax.broadcasted_iota(jnp.int32, (tm, n_buf), 1)
    sel = jnp.zeros((tm, n_buf), F32)
    for k in range(TOP_K):
        sel = jnp.where(r == pos[:, TOP_K + k:TOP_K + k + 1], gates[:, k:k + 1], sel)
    hi, lo = _split_bf16(sel)
    rows = buf[slot]
    y = x_ref[...] + _dot(hi, rows) + _dot(lo, rows)
    y_ref[...] = _rms(y, g_ref[...]) if final else y


def _collect(x1, rows, runs, pos, gates, g, final, row_offset, tm):
    t = x1.shape[0]
    off = row_offset // tm
    n_buf = tm * TOP_K + 2 * N_EXPERTS * RUN_ROWS
    grid_spec = pltpu.PrefetchScalarGridSpec(
        num_scalar_prefetch=3,
        grid=(t // tm,),
        in_specs=[pl.BlockSpec((tm, D_MODEL), lambda i, *_: (i, 0)),
                  pl.BlockSpec((tm, ROUTER_LANES), lambda i, *_: (i + off, 0)),
                  pl.BlockSpec((tm, ROUTER_LANES), lambda i, *_: (i + off, 0)),
                  pl.BlockSpec(g.shape, lambda i, *_: (0, 0)),
                  pl.BlockSpec(memory_space=pl.ANY)],
        out_specs=pl.BlockSpec((tm, D_MODEL), lambda i, *_: (i, 0)),
        scratch_shapes=[pltpu.VMEM((2, n_buf, D_MODEL), BF16), pltpu.SemaphoreType.DMA((2,))],
    )
    return pl.pallas_call(
        functools.partial(_collect_kernel, final=final, tile_off=off),
        grid_spec=grid_spec,
        out_shape=jax.ShapeDtypeStruct((t, D_MODEL), F32),
        compiler_params=_params(("arbitrary",)),
        name="collect",
    )(*runs, x1, pos, gates, g, rows)


def _lane_prefix_sum(x):
    lane = lax.broadcasted_iota(jnp.int32, x.shape, 1)
    shift = 1
    while shift < ROUTER_LANES:
        x = x + jnp.where(lane >= shift, pltpu.roll(x, shift, axis=1), 0.0)
        shift *= 2
    return x


def _route_kernel(lg_ref, slots_ref, gates_ref, counts_ref, runs_ref, base_ref, pstart_ref):
    phase = pl.program_id(0)
    i = pl.program_id(1)
    tm = lg_ref.shape[0]
    lane = lax.broadcasted_iota(jnp.int32, (tm, ROUTER_LANES), 1)
    lane_f = lane.astype(F32)
    x = jnp.where(lane < N_EXPERTS, lg_ref[...], -jnp.inf)
    vals, idxs = [], []
    sel = jnp.zeros((tm, ROUTER_LANES), F32)
    for _ in range(TOP_K):
        m = jnp.max(x, axis=-1, keepdims=True)
        idx = jnp.min(jnp.where(x == m, lane_f, float(ROUTER_LANES)), axis=-1, keepdims=True)
        hit = lane_f == idx
        x = jnp.where(hit, -jnp.inf, x)
        sel = jnp.where(hit, 1.0, sel)
        vals.append(m)
        idxs.append(idx)
    tile_counts = jnp.sum(sel, axis=0, keepdims=True)

    @pl.when(jnp.logical_and(phase == 0, i == 0))
    def _():
        base_ref[...] = jnp.zeros_like(base_ref)

    @pl.when(phase == 0)
    def _():
        base_ref[...] = base_ref[...] + tile_counts

    @pl.when(jnp.logical_and(phase == 1, i == 0))
    def _():
        counts = base_ref[...]
        counts_ref[...] = counts
        padded = jnp.ceil(counts * (1.0 / MOE_BLOCK)) * MOE_BLOCK
        pstart_ref[...] = _lane_prefix_sum(padded) - padded
        base_ref[...] = jnp.zeros_like(base_ref)

    @pl.when(phase == 1)
    def _():
        r = lax.broadcasted_iota(jnp.int32, (tm, tm), 0)
        c = lax.broadcasted_iota(jnp.int32, (tm, tm), 1)
        earlier = jnp.where(c < r, 1.0, 0.0).astype(BF16)
        run_start = base_ref[...] + pstart_ref[...]
        pos = _dot(earlier, sel.astype(BF16)) + run_start[0:1, :]
        aligned = jnp.floor(run_start * (1.0 / RUN_ROWS)) * RUN_ROWS
        pieces = jnp.where(tile_counts > 0.0, jnp.ceil((run_start - aligned + tile_counts) * (1.0 / RUN_ROWS)), 0.0)
        buf_base = (_lane_prefix_sum(pieces) - pieces) * RUN_ROWS
        to_buf = (buf_base - aligned)[0:1, :]
        weights = [jnp.exp(v - vals[0]) for v in vals]
        total = functools.reduce(lambda a, b: a + b, weights)
        slots = jnp.zeros((tm, ROUTER_LANES), jnp.int32)
        gates = jnp.zeros((tm, ROUTER_LANES), F32)
        for k in range(TOP_K):
            mine = lane_f == idxs[k]
            slot_k = jnp.sum(jnp.where(mine, pos, 0.0), axis=-1, keepdims=True)
            buf_k = jnp.sum(jnp.where(mine, pos + to_buf, 0.0), axis=-1, keepdims=True)
            slots = jnp.where(lane == k, slot_k.astype(jnp.int32), slots)
            slots = jnp.where(lane == TOP_K + k, buf_k.astype(jnp.int32), slots)
            gates = jnp.where(lane == k, weights[k] / total, gates)
        slots_ref[...] = slots
        gates_ref[...] = gates
        sub = lax.broadcasted_iota(jnp.int32, run_start.shape, 0)
        runs = jnp.where(sub == 0, aligned * (1.0 / RUN_ROWS),
                         jnp.where(sub == 1, pieces, buf_base * (1.0 / RUN_ROWS)))
        runs_ref[...] = runs.astype(jnp.int32)
        base_ref[...] = base_ref[...] + tile_counts


def _route(logits):
    t = logits.shape[0]
    tm = math.gcd(TOKEN_TILE, t)
    out_blk = pl.BlockSpec((tm, ROUTER_LANES), lambda p, i: (i * p, 0))
    sub = 8
    return pl.pallas_call(
        _route_kernel,
        grid=(2, t // tm),
        in_specs=[pl.BlockSpec((tm, ROUTER_LANES), lambda p, i: (i, 0))],
        out_specs=[out_blk, out_blk, pl.BlockSpec((sub, ROUTER_LANES), lambda p, i: (0, 0)),
                   pl.BlockSpec((None, sub, ROUTER_LANES), lambda p, i: (i * p, 0, 0))],
        out_shape=[jax.ShapeDtypeStruct((t, ROUTER_LANES), jnp.int32),
                   jax.ShapeDtypeStruct((t, ROUTER_LANES), F32),
                   jax.ShapeDtypeStruct((sub, ROUTER_LANES), F32),
                   jax.ShapeDtypeStruct((t // tm, sub, ROUTER_LANES), jnp.int32)],
        scratch_shapes=[pltpu.VMEM((sub, ROUTER_LANES), F32), pltpu.VMEM((sub, ROUTER_LANES), F32)],
        compiler_params=_params(("arbitrary", "arbitrary")),
        name="route",
    )(logits)


def _block_tables(counts, n_blocks):
    padded = (counts + MOE_BLOCK - 1) // MOE_BLOCK * MOE_BLOCK
    pend = jnp.cumsum(padded)
    bstart = jnp.arange(n_blocks, dtype=jnp.int32) * MOE_BLOCK
    block_exp = jnp.minimum(jnp.sum((pend[None, :] <= bstart[:, None]).astype(jnp.int32), axis=1), N_EXPERTS - 1)
    block_valid = (bstart < pend[-1]).astype(jnp.int32)
    return block_exp, block_valid


def _layer(l, xp, xs, cache_k, cache_v, w):
    bsz, seq, _ = xp.shape
    dbsz, dseq, _ = xs.shape
    tp, ts = bsz * seq, dbsz * dseq

    def dense(x2, n_stream, n_seq, chunk, attn_fn):
        q, k, v, kb, vb, u, zn, znb = _inproj(x2, w['ln_mix_g'][l], w['w_in'][l], w['sgu_ln_g'][l],
                                              w['sgu_ln_b'][l], w['gavg'])
        r3 = lambda a: a.reshape(n_stream, n_seq, A_WIDTH)
        oa = attn_fn(r3(q), r3(kb), r3(vb)).reshape(n_stream * n_seq, A_WIDTH)
        wsp = w['w_spatial'][l][:, :chunk, :chunk]
        bsp = w['b_spatial'][l][:chunk]
        x1, h2, lg = _mixout(x2, oa, u, znb, wsp, bsp, w['out_norm_a_g'][l], w['out_norm_b_g'][l],
                             w['w_out'][l], w['ln_ffn_g'][l], w['wr_hi'][l], w['wr_lo'][l], w['b_router'][l])
        return x1, h2, lg, k, v, zn

    x1p, h2p, lgp, kp, vp, _ = dense(xp.reshape(tp, D_MODEL), bsz, seq, MLP_CHUNK, _attn_prompt)
    x1s, h2s, lgs, ks, vs, zs = dense(xs.reshape(ts, D_MODEL), dbsz, dseq, dseq,
                                      lambda q, k, v: _attn_sample(q, k, v, cache_k, cache_v, l))

    t = tp + ts
    h2 = jnp.concatenate([h2p, h2s, jnp.zeros((1, D_MODEL), BF16)], axis=0)
    slots, gates, counts, runs = _route(jnp.concatenate([lgp, lgs], axis=0))
    route_tile = t // runs.shape[0]
    tok_slot = slots[:, :TOP_K]
    n_rows = t * TOP_K
    n_blocks = -(-n_rows // MOE_BLOCK) + N_EXPERTS + 1
    block_exp, block_valid = _block_tables(counts[0, :N_EXPERTS].astype(jnp.int32), n_blocks)
    flat_tok = jnp.arange(n_rows, dtype=jnp.int32) // TOP_K
    slot_tok = jnp.full((n_blocks * MOE_BLOCK,), t, jnp.int32).at[tok_slot.reshape(-1)].set(flat_tok)
    x_sorted = jnp.take(h2, slot_tok, axis=0, mode='clip')
    out = _moe(block_exp, block_valid, x_sorted, w['w_gate_up'], w['b_gate_up'], w['w_down'], w['b_down'], l)
    run_tables = tuple(runs[:, j, :N_EXPERTS].reshape(-1) for j in range(3))
    final = l == w['depth'] - 1
    yp = _collect(x1p, out, run_tables, slots, gates, w['final_norm_g'], final, 0, route_tile)
    ys = _collect(x1s, out, run_tables, slots, gates, w['final_norm_g'], final, tp, route_tile)
    to_heads = lambda a, n_stream, n_seq: a.reshape(n_stream, n_seq, N_A_HEADS, HEAD_DIM)
    return (yp.reshape(bsz, seq, D_MODEL), ys.reshape(dbsz, dseq, D_MODEL),
            to_heads(kp, bsz, seq), to_heads(vp, bsz, seq), to_heads(ks, dbsz, dseq), to_heads(vs, dbsz, dseq),
            zs.reshape(dbsz, dseq, N_B_GROUPS, B_GROUP_DIM))


def _prepare(ln_mix_g, w_in, sgu_ln_g, sgu_ln_b, w_spatial, b_spatial, out_norm_a_g, out_norm_b_g, w_out,
             ln_ffn_g, w_router, b_router, w_gate_up, b_gate_up, w_down, b_down, final_norm_g):
    depth = w_in.shape[0]
    row = lambda a: a.reshape(depth, 1, -1)
    blk = jnp.arange(MLP_CHUNK, dtype=jnp.int32) // CHUNK
    mask = blk[None, :] <= blk[:, None]
    grp = jnp.arange(B_WIDTH, dtype=jnp.int32) // B_GROUP_DIM
    gavg = jnp.where(grp[:, None] == grp[None, :], 1.0 / B_GROUP_DIM, 0.0).astype(BF16)
    wr = jnp.pad(w_router, ((0, 0), (0, 0), (0, ROUTER_LANES - N_EXPERTS)))
    wr_hi = wr.astype(BF16)
    wr_lo = (wr - wr_hi.astype(F32)).astype(BF16)
    return dict(
        depth=depth,
        ln_mix_g=row(ln_mix_g), w_in=w_in.astype(BF16), sgu_ln_g=row(sgu_ln_g), sgu_ln_b=row(sgu_ln_b),
        gavg=gavg,
        w_spatial=jnp.where(mask[None, None], w_spatial, 0.0).astype(BF16),
        b_spatial=jnp.repeat(jnp.swapaxes(b_spatial, 1, 2), B_GROUP_DIM, axis=2),
        out_norm_a_g=row(out_norm_a_g), out_norm_b_g=row(out_norm_b_g), w_out=w_out.astype(BF16),
        ln_ffn_g=row(ln_ffn_g), wr_hi=wr_hi, wr_lo=wr_lo,
        b_router=row(jnp.pad(b_router, ((0, 0), (0, ROUTER_LANES - N_EXPERTS)))),
        w_gate_up=w_gate_up, b_gate_up=b_gate_up[:, :, None, :],
        w_down=w_down, b_down=b_down[:, :, None, :],
        final_norm_g=final_norm_g.reshape(1, -1),
    )


def kernel(x_prompt, x_sample, cache_k, cache_v, ln_mix_g, w_in, sgu_ln_g, sgu_ln_b, w_spatial, b_spatial,
           out_norm_a_g, out_norm_b_g, w_out, ln_ffn_g, w_router, b_router, w_gate_up, b_gate_up, w_down,
           b_down, final_norm_g):
    w = _prepare(ln_mix_g, w_in, sgu_ln_g, sgu_ln_b, w_spatial, b_spatial, out_norm_a_g, out_norm_b_g, w_out,
                 ln_ffn_g, w_router, b_router, w_gate_up, b_gate_up, w_down, b_down, final_norm_g)
    depth, dbsz, past = cache_k.shape[:3]
    ck = cache_k.reshape(depth, dbsz, past, A_WIDTH)
    cv = cache_v.reshape(depth, dbsz, past, A_WIDTH)
    xp, xs = x_prompt, x_sample
    kps, vps, kss, vss, zss = [], [], [], [], []
    for l in range(depth):
        xp, xs, kp, vp, ks, vs, zs = _layer(l, xp, xs, ck, cv, w)
        kps.append(kp)
        vps.append(vp)
        kss.append(ks)
        vss.append(vs)
        zss.append(zs)
    return (xp, xs, jnp.stack(kps), jnp.stack(vps), jnp.stack(kss), jnp.stack(vss), jnp.stack(zss))
```

```python
import functools
import math

import jax
import jax.numpy as jnp
from jax import lax
from jax.experimental import pallas as pl
from jax.experimental.pallas import tpu as pltpu

F32 = jnp.float32
BF16 = jnp.bfloat16

D_MODEL = 1024
HEAD_DIM = 64
N_A_HEADS = 8
A_WIDTH = N_A_HEADS * HEAD_DIM
N_B_GROUPS = 8
B_GROUP_DIM = 64
B_WIDTH = N_B_GROUPS * B_GROUP_DIM
CHUNK = 64
MLP_CHUNK = 128
SB_SCALE = HEAD_DIM ** -0.5
N_EXPERTS = 32
TOP_K = 4
D_FF = D_MODEL
SWIGLU_LIMIT = 7.0
SWIGLU_ALPHA = 1.702
NORM_EPS = 1e-5

LANES = 128
HEADS_PER_TILE = LANES // HEAD_DIM
N_HEAD_TILES = A_WIDTH // LANES
ROUTER_LANES = 128
VMEM_LIMIT = 56 * 1024 * 1024
TOKEN_TILE = 256
ATTN_Q_TILE = 256
ATTN_K_TILE = 256
MOE_BLOCK = 512
WEIGHT_CAST_ROWS = 128
FF_CHUNK = 256
RUN_ROWS = 16
DISPATCH_TILES = 4
DISPATCH_CHUNK = 512
ZERO_WEIGHT_LOG = -110.0


def _params(sem):
    return pltpu.CompilerParams(dimension_semantics=sem, vmem_limit_bytes=VMEM_LIMIT)


def _split_bf16(t):
    hi = t.astype(BF16)
    lo = (t - hi.astype(F32)).astype(BF16)
    return hi, lo


def _dot(a, b):
    return jnp.dot(a, b, preferred_element_type=F32)


def _dot_nt(a, b):
    return lax.dot_general(a, b, (((1,), (1,)), ((), ())), preferred_element_type=F32)


def _rms(x, g):
    return x * lax.rsqrt(jnp.mean(x * x, axis=-1, keepdims=True) + NORM_EPS) * g


def _inproj_kernel(x_ref, g_ref, w_ref, lng_ref, lnb_ref, gavg_ref,
                   q_ref, k_ref, v_ref, kb_ref, vb_ref, u_ref, zn_ref, znb_ref):
    hn = _rms(x_ref[...], g_ref[...]).astype(BF16)

    def proj(sec):
        return _dot(hn, w_ref[:, sec * A_WIDTH:(sec + 1) * A_WIDTH])

    q_ref[...] = (proj(0) * SB_SCALE).astype(BF16)
    k = proj(1)
    k_ref[...] = k
    kb_ref[...] = k.astype(BF16)
    v = proj(2)
    v_ref[...] = v
    vb_ref[...] = v.astype(BF16)
    u_ref[...] = jax.nn.gelu(proj(3)).astype(BF16)
    z = jax.nn.gelu(proj(4))
    gavg = gavg_ref[...]

    def group_mean(t):
        hi, lo = _split_bf16(t)
        return _dot(hi, gavg) + _dot(lo, gavg)

    zc = z - group_mean(z)
    var = group_mean(zc * zc)
    zn = zc * lax.rsqrt(var + NORM_EPS) * lng_ref[...] + lnb_ref[...]
    zn_ref[...] = zn
    znb_ref[...] = zn.astype(BF16)


def _inproj(x, g, w_bf, lng, lnb, gavg):
    t = x.shape[0]
    tm = min(TOKEN_TILE, t)
    row = lambda w: pl.BlockSpec((tm, w), lambda i: (i, 0))
    full = lambda a: pl.BlockSpec(a.shape, lambda i: (0,) * a.ndim)
    sds = lambda w, dt: jax.ShapeDtypeStruct((t, w), dt)
    return pl.pallas_call(
        _inproj_kernel,
        grid=(t // tm,),
        in_specs=[row(D_MODEL), full(g), full(w_bf), full(lng), full(lnb), full(gavg)],
        out_specs=[row(A_WIDTH)] * 8,
        out_shape=[sds(A_WIDTH, BF16), sds(A_WIDTH, F32), sds(A_WIDTH, F32), sds(A_WIDTH, BF16),
                   sds(A_WIDTH, BF16), sds(B_WIDTH, BF16), sds(B_WIDTH, F32), sds(B_WIDTH, BF16)],
        compiler_params=_params(("parallel",)),
        name="inproj",
    )(x, g, w_bf, lng, lnb, gavg)


def _log_terms(z):
    soft = jnp.log(1.0 + jnp.exp(-jnp.abs(z)))
    return jnp.minimum(z, 0.0) - soft, jnp.minimum(-z, 0.0) - soft


def _tri(n):
    j = lax.broadcasted_iota(jnp.int32, (n, n), 0)
    s = lax.broadcasted_iota(jnp.int32, (n, n), 1)
    return jnp.where(j > s, 1.0, 0.0).astype(BF16)


def _later_sum(log_keep, tri):
    hi, lo = _split_bf16(log_keep)
    return _dot(hi, tri) + _dot(lo, tri)


def _attn_kernel(q_ref, kd_ref, vd_ref, kp_ref, vp_ref, o_ref, *, n_past_fn, tk):
    tq = q_ref.shape[0]
    q = q_ref[...].astype(F32)
    lane = lax.broadcasted_iota(jnp.int32, (tq, LANES), 1)
    qs = [jnp.where((lane // HEAD_DIM) == h, q, 0.0).astype(BF16) for h in range(HEADS_PER_TILE)]

    kd = kd_ref[...].astype(BF16)
    vd = vd_ref[...].astype(BF16)
    t_pos = lax.broadcasted_iota(jnp.int32, (tq, tq), 0)
    s_pos = lax.broadcasted_iota(jnp.int32, (tq, tq), 1)
    visible = s_pos < t_pos
    tri_d = _tri(tq)
    accs, carries = [], []
    for qh in qs:
        log_beta, log_keep = _log_terms(_dot_nt(qh, kd))
        log_keep = jnp.where(visible, log_keep, 0.0)
        a = jnp.where(visible, jnp.exp(log_beta + _later_sum(log_keep, tri_d)), 0.0)
        accs.append(_dot(a.astype(BF16), vd))
        carries.append(jnp.sum(log_keep, axis=-1, keepdims=True))

    n_past = n_past_fn()
    tri_p = _tri(tk)

    def carry_max(carries):
        return functools.reduce(jnp.maximum, [jnp.max(c) for c in carries])

    def cond(state):
        j, _, _, cmax = state
        return jnp.logical_and(j < n_past, cmax > ZERO_WEIGHT_LOG)

    def past_block(j, accs, carries, live=None):
        start = pl.multiple_of(jnp.maximum(n_past - 1 - j, 0) * tk, tk)
        kp = kp_ref[pl.ds(start, tk), :].astype(BF16)
        vp = vp_ref[pl.ds(start, tk), :].astype(BF16)
        new_accs, new_carries = [], []
        for qh, acc, carry in zip(qs, accs, carries):
            log_beta, log_keep = _log_terms(_dot_nt(qh, kp))
            a = jnp.exp(log_beta + _later_sum(log_keep, tri_p) + carry)
            av = _dot(a.astype(BF16), vp)
            new_accs.append(acc + (av if live is None else jnp.where(live, av, 0.0)))
            new_carries.append(carry + jnp.sum(log_keep, axis=-1, keepdims=True))
        return tuple(new_accs), tuple(new_carries)

    def body(state):
        j, accs, carries, _ = state
        accs, carries = past_block(j, accs, carries)
        return j + 1, accs, carries, carry_max(carries)

    accs, carries = past_block(0, accs, carries, live=n_past > 0)
    _, accs, _, _ = lax.while_loop(cond, body, (jnp.int32(1), accs, carries, carry_max(carries)))
    out = accs[0]
    for h in range(1, HEADS_PER_TILE):
        out = jnp.where((lane // HEAD_DIM) == h, accs[h], out)
    o_ref[...] = out


def _attn_prompt(q_bf, k_bf, v_bf):
    b, s, _ = q_bf.shape
    tq = min(ATTN_Q_TILE, s)
    tk = tq
    blk = pl.BlockSpec((None, tq, LANES), lambda bi, hp, i: (bi, i, hp))
    seq = pl.BlockSpec((None, s, LANES), lambda bi, hp, i: (bi, 0, hp))
    kern = functools.partial(_attn_kernel, n_past_fn=lambda: pl.program_id(2) * (tq // tk), tk=tk)
    return pl.pallas_call(
        kern,
        grid=(b, N_HEAD_TILES, s // tq),
        in_specs=[blk, blk, blk, seq, seq],
        out_specs=blk,
        out_shape=jax.ShapeDtypeStruct((b, s, A_WIDTH), F32),
        compiler_params=_params(("parallel", "parallel", "arbitrary")),
        name="attn_prompt",
    )(q_bf, k_bf, v_bf, k_bf, v_bf)


def _attn_sample(q_bf, k_bf, v_bf, cache_k, cache_v, layer):
    b, n, _ = q_bf.shape
    p = cache_k.shape[2]
    tk = min(ATTN_K_TILE, p)
    blk = pl.BlockSpec((None, n, LANES), lambda bi, hp: (bi, 0, hp))
    past = pl.BlockSpec((None, None, p, LANES), lambda bi, hp: (layer, bi, 0, hp))
    kern = functools.partial(_attn_kernel, n_past_fn=lambda: p // tk, tk=tk)
    return pl.pallas_call(
        kern,
        grid=(b, N_HEAD_TILES),
        in_specs=[blk, blk, blk, past, past],
        out_specs=blk,
        out_shape=jax.ShapeDtypeStruct((b, n, A_WIDTH), F32),
        compiler_params=_params(("parallel", "parallel")),
        name="attn_sample",
    )(q_bf, k_bf, v_bf, cache_k, cache_v)


def _mixout_kernel(x_ref, oa_ref, u_ref, znb_ref, wsp_ref, bsp_ref, ga_ref, gb_ref, wout_ref,
                   lnf_ref, wrh_ref, wrl_ref, br_ref, x1_ref, h2_ref, lg_ref, mix_ref):
    tm = x_ref.shape[0]
    chunk = wsp_ref.shape[1]
    group = lax.broadcasted_iota(jnp.int32, (chunk, B_WIDTH), 1) // B_GROUP_DIM
    for c in range(tm // chunk):
        zc = znb_ref[c * chunk:(c + 1) * chunk, :]
        m = jnp.zeros((chunk, B_WIDTH), F32)
        for g in range(N_B_GROUPS):
            m = jnp.where(group == g, _dot(wsp_ref[g], zc), m)
        mix_ref[c * chunk:(c + 1) * chunk, :] = m + bsp_ref[...]
    ob = u_ref[...].astype(F32) * mix_ref[...]
    ya = _rms(oa_ref[...], ga_ref[...]).astype(BF16)
    yb = _rms(ob, gb_ref[...]).astype(BF16)
    x1 = x_ref[...] + _dot(ya, wout_ref[:A_WIDTH, :]) + _dot(yb, wout_ref[A_WIDTH:, :])
    x1_ref[...] = x1
    h2 = _rms(x1, lnf_ref[...])
    hi, lo = _split_bf16(h2)
    h2_ref[...] = hi
    wrh = wrh_ref[...]
    lg_ref[...] = _dot(hi, wrh) + _dot(lo, wrh) + _dot(hi, wrl_ref[...]) + br_ref[...]


def _mixout(x, oa, u, znb, wsp, bsp, ga, gb, wout_bf, lnf, wrh, wrl, br):
    t = x.shape[0]
    tm = min(TOKEN_TILE, t)
    row = lambda w: pl.BlockSpec((tm, w), lambda i: (i, 0))
    full = lambda a: pl.BlockSpec(a.shape, lambda i: (0,) * a.ndim)
    return pl.pallas_call(
        _mixout_kernel,
        grid=(t // tm,),
        in_specs=[row(D_MODEL), row(A_WIDTH), row(B_WIDTH), row(B_WIDTH), full(wsp), full(bsp),
                  full(ga), full(gb), full(wout_bf), full(lnf), full(wrh), full(wrl), full(br)],
        out_specs=[row(D_MODEL), row(D_MODEL), row(ROUTER_LANES)],
        out_shape=[jax.ShapeDtypeStruct((t, D_MODEL), F32), jax.ShapeDtypeStruct((t, D_MODEL), BF16),
                   jax.ShapeDtypeStruct((t, ROUTER_LANES), F32)],
        scratch_shapes=[pltpu.VMEM((tm, B_WIDTH), F32)],
        compiler_params=_params(("parallel",)),
        name="mixout",
    )(x, oa, u, znb, wsp, bsp, ga, gb, wout_bf, lnf, wrh, wrl, br)


def _moe_kernel(bexp_ref, bval_ref, xs_ref, wgu_ref, bgu_ref, wd_ref, bd_ref, o_ref, wgu_bf, wd_bf):
    i = pl.program_id(0)
    valid = bval_ref[i] > 0
    new_expert = jnp.logical_or(i == 0, bexp_ref[i] != bexp_ref[jnp.maximum(i - 1, 0)])

    @pl.when(jnp.logical_and(valid, new_expert))
    def _():
        def cast_rows(r, _):
            rows = pl.ds(pl.multiple_of(r * WEIGHT_CAST_ROWS, WEIGHT_CAST_ROWS), WEIGHT_CAST_ROWS)
            wgu_bf[rows, :] = wgu_ref[rows, :].astype(BF16)
            wd_bf[rows, :] = wd_ref[rows, :].astype(BF16)
            return 0
        lax.fori_loop(0, D_MODEL // WEIGHT_CAST_ROWS, cast_rows, 0)

    @pl.when(valid)
    def _():
        xs = xs_ref[...]
        acc = None
        for c in range(D_FF // FF_CHUNK):
            lo, hi = c * FF_CHUNK, (c + 1) * FF_CHUNK
            gate = _dot(xs, wgu_bf[:, lo:hi]) + bgu_ref[:, lo:hi]
            lin = _dot(xs, wgu_bf[:, D_FF + lo:D_FF + hi]) + bgu_ref[:, D_FF + lo:D_FF + hi]
            gate = jnp.minimum(gate, SWIGLU_LIMIT)
            lin = jnp.clip(lin, -SWIGLU_LIMIT, SWIGLU_LIMIT)
            act = gate * jax.nn.sigmoid(SWIGLU_ALPHA * gate) * (lin + 1.0)
            part = _dot(act.astype(BF16), wd_bf[lo:hi, :])
            acc = part if acc is None else acc + part
        o_ref[...] = (acc + bd_ref[...]).astype(o_ref.dtype)

    @pl.when(bval_ref[i] == 0)
    def _():
        o_ref[...] = jnp.zeros_like(o_ref)


def _moe(block_exp, block_valid, xs, wgu, bgu, wd, bd, layer):
    n_slots = xs.shape[0]
    bm = MOE_BLOCK
    grid_spec = pltpu.PrefetchScalarGridSpec(
        num_scalar_prefetch=2,
        grid=(n_slots // bm,),
        in_specs=[
            pl.BlockSpec((bm, D_MODEL), lambda i, be, bv: (i, 0)),
            pl.BlockSpec((None, None, D_MODEL, 2 * D_FF), lambda i, be, bv: (layer, be[i], 0, 0)),
            pl.BlockSpec((None, None, 1, 2 * D_FF), lambda i, be, bv: (layer, be[i], 0, 0)),
            pl.BlockSpec((None, None, D_FF, D_MODEL), lambda i, be, bv: (layer, be[i], 0, 0)),
            pl.BlockSpec((None, None, 1, D_MODEL), lambda i, be, bv: (layer, be[i], 0, 0)),
        ],
        out_specs=pl.BlockSpec((bm, D_MODEL), lambda i, be, bv: (i, 0)),
        scratch_shapes=[pltpu.VMEM((D_MODEL, 2 * D_FF), BF16), pltpu.VMEM((D_FF, D_MODEL), BF16)],
    )
    return pl.pallas_call(
        _moe_kernel,
        grid_spec=grid_spec,
        out_shape=jax.ShapeDtypeStruct((n_slots, D_MODEL), BF16),
        compiler_params=_params(("arbitrary",)),
        name="moe",
    )(block_exp, block_valid, xs, wgu, bgu, wd, bd)


def _collect_kernel(src_ref, cnt_ref, dst_ref, x_ref, pos_ref, gates_ref, g_ref, rows_hbm, y_ref, buf, sem,
                    *, final, tile_off):
    i = pl.program_id(0)
    n = pl.num_programs(0)

    def run_copies(tile, slot, wait):
        def per_expert(e, _):
            idx = (tile + tile_off) * N_EXPERTS + e
            src, dst = src_ref[idx], dst_ref[idx]

            def per_piece(c, _):
                cp = pltpu.make_async_copy(
                    rows_hbm.at[pl.ds(pl.multiple_of((src + c) * RUN_ROWS, RUN_ROWS), RUN_ROWS), :],
                    buf.at[slot, pl.ds(pl.multiple_of((dst + c) * RUN_ROWS, RUN_ROWS), RUN_ROWS), :],
                    sem.at[slot])
                if wait:
                    cp.wait()
                else:
                    cp.start()
                return 0

            lax.fori_loop(0, cnt_ref[idx], per_piece, 0)
            return 0

        lax.fori_loop(0, N_EXPERTS, per_expert, 0)

    @pl.when(i == 0)
    def _():
        buf[...] = jnp.zeros_like(buf)
        run_copies(0, 0, wait=False)

    @pl.when(i + 1 < n)
    def _():
        run_copies(i + 1, (i + 1) % 2, wait=False)

    slot = i % 2
    run_copies(i, slot, wait=True)

    tm = x_ref.shape[0]
    n_buf = buf.shape[1]
    pos = pos_ref[...]
    gates = gates_ref[...]
    r = lax.broadcasted_iota(jnp.int32, (tm, n_buf), 1)
    sel = jnp.zeros((tm, n_buf), F32)
    for k in range(TOP_K):
        sel = jnp.where(r == pos[:, TOP_K + k:TOP_K + k + 1], gates[:, k:k + 1], sel)
    hi, lo = _split_bf16(sel)
    rows = buf[slot]
    y = x_ref[...] + _dot(hi, rows) + _dot(lo, rows)
    y_ref[...] = _rms(y, g_ref[...]) if final else y


def _collect(x1, rows, runs, pos, gates, g, final, row_offset, tm):
    t = x1.shape[0]
    off = row_offset // tm
    n_buf = tm * TOP_K + 2 * N_EXPERTS * RUN_ROWS
    grid_spec = pltpu.PrefetchScalarGridSpec(
        num_scalar_prefetch=3,
        grid=(t // tm,),
        in_specs=[pl.BlockSpec((tm, D_MODEL), lambda i, *_: (i, 0)),
                  pl.BlockSpec((tm, ROUTER_LANES), lambda i, *_: (i + off, 0)),
                  pl.BlockSpec((tm, ROUTER_LANES), lambda i, *_: (i + off, 0)),
                  pl.BlockSpec(g.shape, lambda i, *_: (0, 0)),
                  pl.BlockSpec(memory_space=pl.ANY)],
        out_specs=pl.BlockSpec((tm, D_MODEL), lambda i, *_: (i, 0)),
        scratch_shapes=[pltpu.VMEM((2, n_buf, D_MODEL), BF16), pltpu.SemaphoreType.DMA((2,))],
    )
    return pl.pallas_call(
        functools.partial(_collect_kernel, final=final, tile_off=off),
        grid_spec=grid_spec,
        out_shape=jax.ShapeDtypeStruct((t, D_MODEL), F32),
        compiler_params=_params(("arbitrary",)),
        name="collect",
    )(*runs, x1, pos, gates, g, rows)


def _lane_prefix_sum(x):
    lane = lax.broadcasted_iota(jnp.int32, x.shape, 1)
    shift = 1
    while shift < ROUTER_LANES:
        x = x + jnp.where(lane >= shift, pltpu.roll(x, shift, axis=1), 0.0)
        shift *= 2
    return x


def _route_kernel(lg_ref, slots_ref, gates_ref, counts_ref, runs_ref, group_runs_ref,
                  base_ref, pstart_ref, group_start_ref, group_buf_ref, hist_ref):
    phase = pl.program_id(0)
    i = pl.program_id(1)
    group = i // DISPATCH_TILES
    first_in_group = i % DISPATCH_TILES == 0
    pieces_of = lambda rows: jnp.ceil(rows * (1.0 / RUN_ROWS))
    tm = lg_ref.shape[0]
    lane = lax.broadcasted_iota(jnp.int32, (tm, ROUTER_LANES), 1)
    lane_f = lane.astype(F32)
    x = jnp.where(lane < N_EXPERTS, lg_ref[...], -jnp.inf)
    vals, idxs = [], []
    sel = jnp.zeros((tm, ROUTER_LANES), F32)
    for _ in range(TOP_K):
        m = jnp.max(x, axis=-1, keepdims=True)
        idx = jnp.min(jnp.where(x == m, lane_f, float(ROUTER_LANES)), axis=-1, keepdims=True)
        hit = lane_f == idx
        x = jnp.where(hit, -jnp.inf, x)
        sel = jnp.where(hit, 1.0, sel)
        vals.append(m)
        idxs.append(idx)
    tile_counts = jnp.sum(sel, axis=0, keepdims=True)

    @pl.when(jnp.logical_and(phase == 0, i == 0))
    def _():
        base_ref[...] = jnp.zeros_like(base_ref)

    @pl.when(jnp.logical_and(phase == 0, first_in_group))
    def _():
        base_ref[...] = pieces_of(base_ref[...]) * RUN_ROWS
        hist_ref[group] = jnp.zeros_like(base_ref)

    @pl.when(phase == 0)
    def _():
        base_ref[...] = base_ref[...] + tile_counts
        hist_ref[group] = hist_ref[group] + tile_counts

    @pl.when(jnp.logical_and(phase == 1, i == 0))
    def _():
        counts = base_ref[...]
        counts_ref[...] = counts
        padded = jnp.ceil(counts * (1.0 / MOE_BLOCK)) * MOE_BLOCK
        pstart_ref[...] = _lane_prefix_sum(padded) - padded
        base_ref[...] = jnp.zeros_like(base_ref)

    @pl.when(jnp.logical_and(phase == 1, first_in_group))
    def _():
        base = pieces_of(base_ref[...]) * RUN_ROWS
        base_ref[...] = base
        group_start = base + pstart_ref[...]
        group_pieces = pieces_of(hist_ref[group])
        group_buf = (_lane_prefix_sum(group_pieces) - group_pieces) * RUN_ROWS
        group_start_ref[...] = group_start
        group_buf_ref[...] = group_buf
        sub = lax.broadcasted_iota(jnp.int32, base.shape, 0)
        table = jnp.where(sub == 0, group_start * (1.0 / RUN_ROWS),
                          jnp.where(sub == 1, group_pieces, group_buf * (1.0 / RUN_ROWS)))
        group_runs_ref[...] = table.astype(jnp.int32)

    @pl.when(phase == 1)
    def _():
        r = lax.broadcasted_iota(jnp.int32, (tm, tm), 0)
        c = lax.broadcasted_iota(jnp.int32, (tm, tm), 1)
        earlier = jnp.where(c < r, 1.0, 0.0).astype(BF16)
        run_start = base_ref[...] + pstart_ref[...]
        pos = _dot(earlier, sel.astype(BF16)) + run_start[0:1, :]
        aligned = jnp.floor(run_start * (1.0 / RUN_ROWS)) * RUN_ROWS
        pieces = jnp.where(tile_counts > 0.0, jnp.ceil((run_start - aligned + tile_counts) * (1.0 / RUN_ROWS)), 0.0)
        buf_base = (_lane_prefix_sum(pieces) - pieces) * RUN_ROWS
        to_buf = (buf_base - aligned)[0:1, :]
        to_group_buf = (group_buf_ref[...] - group_start_ref[...])[0:1, :]
        weights = [jnp.exp(v - vals[0]) for v in vals]
        total = functools.reduce(lambda a, b: a + b, weights)
        slots = jnp.zeros((tm, ROUTER_LANES), jnp.int32)
        gates = jnp.zeros((tm, ROUTER_LANES), F32)
        for k in range(TOP_K):
            mine = lane_f == idxs[k]
            slot_k = jnp.sum(jnp.where(mine, pos, 0.0), axis=-1, keepdims=True)
            buf_k = jnp.sum(jnp.where(mine, pos + to_buf, 0.0), axis=-1, keepdims=True)
            group_buf_k = jnp.sum(jnp.where(mine, pos + to_group_buf, 0.0), axis=-1, keepdims=True)
            slots = jnp.where(lane == k, slot_k.astype(jnp.int32), slots)
            slots = jnp.where(lane == TOP_K + k, buf_k.astype(jnp.int32), slots)
            slots = jnp.where(lane == 2 * TOP_K + k, group_buf_k.astype(jnp.int32), slots)
            gates = jnp.where(lane == k, weights[k] / total, gates)
        slots_ref[...] = slots
        gates_ref[...] = gates
        sub = lax.broadcasted_iota(jnp.int32, run_start.shape, 0)
        runs = jnp.where(sub == 0, aligned * (1.0 / RUN_ROWS),
                         jnp.where(sub == 1, pieces, buf_base * (1.0 / RUN_ROWS)))
        runs_ref[...] = runs.astype(jnp.int32)
        base_ref[...] = base_ref[...] + tile_counts


def _route(logits):
    t = logits.shape[0]
    tm = math.gcd(TOKEN_TILE, t)
    n_tiles = t // tm
    n_groups = -(-n_tiles // DISPATCH_TILES)
    out_blk = pl.BlockSpec((tm, ROUTER_LANES), lambda p, i: (i * p, 0))
    sub = 8
    vec = pltpu.VMEM((sub, ROUTER_LANES), F32)
    return pl.pallas_call(
        _route_kernel,
        grid=(2, n_tiles),
        in_specs=[pl.BlockSpec((tm, ROUTER_LANES), lambda p, i: (i, 0))],
        out_specs=[out_blk, out_blk, pl.BlockSpec((sub, ROUTER_LANES), lambda p, i: (0, 0)),
                   pl.BlockSpec((None, sub, ROUTER_LANES), lambda p, i: (i * p, 0, 0)),
                   pl.BlockSpec((None, sub, ROUTER_LANES), lambda p, i: (i // DISPATCH_TILES * p, 0, 0))],
        out_shape=[jax.ShapeDtypeStruct((t, ROUTER_LANES), jnp.int32),
                   jax.ShapeDtypeStruct((t, ROUTER_LANES), F32),
                   jax.ShapeDtypeStruct((sub, ROUTER_LANES), F32),
                   jax.ShapeDtypeStruct((n_tiles, sub, ROUTER_LANES), jnp.int32),
                   jax.ShapeDtypeStruct((n_groups, sub, ROUTER_LANES), jnp.int32)],
        scratch_shapes=[vec, vec, vec, vec, pltpu.VMEM((n_groups, sub, ROUTER_LANES), F32)],
        compiler_params=_params(("arbitrary", "arbitrary")),
        name="route",
    )(logits)


def _dispatch_kernel(dst_ref, cnt_ref, src_ref, h_ref, pos_ref, blank_hbm, xs_hbm, buf, sem, *, n_tokens):
    del blank_hbm
    g = pl.program_id(0)
    n = pl.num_programs(0)
    tg = h_ref.shape[0]
    n_buf = buf.shape[1]
    slot = g % 2

    def run_copies(grp, slot, wait):
        def per_expert(e, _):
            idx = grp * N_EXPERTS + e
            dst, src = dst_ref[idx], src_ref[idx]

            def per_piece(c, _):
                cp = pltpu.make_async_copy(
                    buf.at[slot, pl.ds(pl.multiple_of((src + c) * RUN_ROWS, RUN_ROWS), RUN_ROWS), :],
                    xs_hbm.at[pl.ds(pl.multiple_of((dst + c) * RUN_ROWS, RUN_ROWS), RUN_ROWS), :],
                    sem.at[slot])
                if wait:
                    cp.wait()
                else:
                    cp.start()
                return 0

            lax.fori_loop(0, cnt_ref[idx], per_piece, 0)
            return 0

        lax.fori_loop(0, N_EXPERTS, per_expert, 0)

    @pl.when(g >= 2)
    def _():
        run_copies(g - 2, slot, wait=True)

    pos_t = pos_ref[...].T
    tok = g * tg + lax.broadcasted_iota(jnp.int32, (1, tg), 1)
    rows_of = [jnp.where(tok < n_tokens, pos_t[2 * TOP_K + k:2 * TOP_K + k + 1, :], -1) for k in range(TOP_K)]
    h = h_ref[...]
    for c in range(n_buf // DISPATCH_CHUNK):
        r = c * DISPATCH_CHUNK + lax.broadcasted_iota(jnp.int32, (DISPATCH_CHUNK, tg), 0)
        sel = jnp.zeros((DISPATCH_CHUNK, tg), F32)
        for k in range(TOP_K):
            sel = jnp.where(r == rows_of[k], 1.0, sel)
        buf[slot, c * DISPATCH_CHUNK:(c + 1) * DISPATCH_CHUNK, :] = _dot(sel.astype(BF16), h).astype(BF16)

    run_copies(g, slot, wait=False)

    @pl.when(jnp.logical_and(g == n - 1, g >= 1))
    def _():
        run_copies(g - 1, 1 - slot, wait=True)

    @pl.when(g == n - 1)
    def _():
        run_copies(g, slot, wait=True)


def _dispatch(h2, pos, group_tables, n_slots, n_tokens, tg):
    n_groups = h2.shape[0] // tg
    n_buf = tg * TOP_K + N_EXPERTS * RUN_ROWS
    grid_spec = pltpu.PrefetchScalarGridSpec(
        num_scalar_prefetch=3,
        grid=(n_groups,),
        in_specs=[pl.BlockSpec((tg, D_MODEL), lambda g, *_: (g, 0)),
                  pl.BlockSpec((tg, ROUTER_LANES), lambda g, *_: (g, 0)),
                  pl.BlockSpec(memory_space=pl.ANY)],
        out_specs=pl.BlockSpec(memory_space=pl.ANY),
        scratch_shapes=[pltpu.VMEM((2, n_buf, D_MODEL), BF16), pltpu.SemaphoreType.DMA((2,))],
    )
    blank = jnp.zeros((n_slots, D_MODEL), BF16)
    return pl.pallas_call(
        functools.partial(_dispatch_kernel, n_tokens=n_tokens),
        grid_spec=grid_spec,
        out_shape=jax.ShapeDtypeStruct((n_slots, D_MODEL), BF16),
        input_output_aliases={5: 0},
        compiler_params=_params(("arbitrary",)),
        name="dispatch",
    )(*group_tables, h2, pos, blank)


def _block_tables(counts, n_blocks):
    padded = (counts + MOE_BLOCK - 1) // MOE_BLOCK * MOE_BLOCK
    pend = jnp.cumsum(padded)
    bstart = jnp.arange(n_blocks, dtype=jnp.int32) * MOE_BLOCK
    block_exp = jnp.minimum(jnp.sum((pend[None, :] <= bstart[:, None]).astype(jnp.int32), axis=1), N_EXPERTS - 1)
    block_valid = (bstart < pend[-1]).astype(jnp.int32)
    return block_exp, block_valid


def _layer(l, xp, xs, cache_k, cache_v, w):
    bsz, seq, _ = xp.shape
    dbsz, dseq, _ = xs.shape
    tp, ts = bsz * seq, dbsz * dseq

    def dense(x2, n_stream, n_seq, chunk, attn_fn):
        q, k, v, kb, vb, u, zn, znb = _inproj(x2, w['ln_mix_g'][l], w['w_in'][l], w['sgu_ln_g'][l],
                                              w['sgu_ln_b'][l], w['gavg'])
        r3 = lambda a: a.reshape(n_stream, n_seq, A_WIDTH)
        oa = attn_fn(r3(q), r3(kb), r3(vb)).reshape(n_stream * n_seq, A_WIDTH)
        wsp = w['w_spatial'][l][:, :chunk, :chunk]
        bsp = w['b_spatial'][l][:chunk]
        x1, h2, lg = _mixout(x2, oa, u, znb, wsp, bsp, w['out_norm_a_g'][l], w['out_norm_b_g'][l],
                             w['w_out'][l], w['ln_ffn_g'][l], w['wr_hi'][l], w['wr_lo'][l], w['b_router'][l])
        return x1, h2, lg, k, v, zn

    x1p, h2p, lgp, kp, vp, _ = dense(xp.reshape(tp, D_MODEL), bsz, seq, MLP_CHUNK, _attn_prompt)
    x1s, h2s, lgs, ks, vs, zs = dense(xs.reshape(ts, D_MODEL), dbsz, dseq, dseq,
                                      lambda q, k, v: _attn_sample(q, k, v, cache_k, cache_v, l))

    t = tp + ts
    slots, gates, counts, runs, group_runs = _route(jnp.concatenate([lgp, lgs], axis=0))
    route_tile = t // runs.shape[0]
    n_groups = group_runs.shape[0]
    tg = route_tile * DISPATCH_TILES
    tables = lambda a: tuple(a[:, j, :N_EXPERTS].reshape(-1) for j in range(3))
    run_tables, group_tables = tables(runs), tables(group_runs)
    max_rows = t * TOP_K + n_groups * N_EXPERTS * (RUN_ROWS - 1)
    n_blocks = -(-max_rows // MOE_BLOCK) + N_EXPERTS + 1
    block_exp, block_valid = _block_tables(counts[0, :N_EXPERTS].astype(jnp.int32), n_blocks)
    h2 = jnp.concatenate([h2p, h2s, jnp.zeros((n_groups * tg - t, D_MODEL), BF16)], axis=0)
    x_sorted = _dispatch(h2, slots, group_tables, n_blocks * MOE_BLOCK, t, tg)
    out = _moe(block_exp, block_valid, x_sorted, w['w_gate_up'], w['b_gate_up'], w['w_down'], w['b_down'], l)
    final = l == w['depth'] - 1
    yp = _collect(x1p, out, run_tables, slots, gates, w['final_norm_g'], final, 0, route_tile)
    ys = _collect(x1s, out, run_tables, slots, gates, w['final_norm_g'], final, tp, route_tile)
    to_heads = lambda a, n_stream, n_seq: a.reshape(n_stream, n_seq, N_A_HEADS, HEAD_DIM)
    return (yp.reshape(bsz, seq, D_MODEL), ys.reshape(dbsz, dseq, D_MODEL),
            to_heads(kp, bsz, seq), to_heads(vp, bsz, seq), to_heads(ks, dbsz, dseq), to_heads(vs, dbsz, dseq),
            zs.reshape(dbsz, dseq, N_B_GROUPS, B_GROUP_DIM))


def _prepare(ln_mix_g, w_in, sgu_ln_g, sgu_ln_b, w_spatial, b_spatial, out_norm_a_g, out_norm_b_g, w_out,
             ln_ffn_g, w_router, b_router, w_gate_up, b_gate_up, w_down, b_down, final_norm_g):
    depth = w_in.shape[0]
    row = lambda a: a.reshape(depth, 1, -1)
    blk = jnp.arange(MLP_CHUNK, dtype=jnp.int32) // CHUNK
    mask = blk[None, :] <= blk[:, None]
    grp = jnp.arange(B_WIDTH, dtype=jnp.int32) // B_GROUP_DIM
    gavg = jnp.where(grp[:, None] == grp[None, :], 1.0 / B_GROUP_DIM, 0.0).astype(BF16)
    wr = jnp.pad(w_router, ((0, 0), (0, 0), (0, ROUTER_LANES - N_EXPERTS)))
    wr_hi = wr.astype(BF16)
    wr_lo = (wr - wr_hi.astype(F32)).astype(BF16)
    return dict(
        depth=depth,
        ln_mix_g=row(ln_mix_g), w_in=w_in.astype(BF16), sgu_ln_g=row(sgu_ln_g), sgu_ln_b=row(sgu_ln_b),
        gavg=gavg,
        w_spatial=jnp.where(mask[None, None], w_spatial, 0.0).astype(BF16),
        b_spatial=jnp.repeat(jnp.swapaxes(b_spatial, 1, 2), B_GROUP_DIM, axis=2),
        out_norm_a_g=row(out_norm_a_g), out_norm_b_g=row(out_norm_b_g), w_out=w_out.astype(BF16),
        ln_ffn_g=row(ln_ffn_g), wr_hi=wr_hi, wr_lo=wr_lo,
        b_router=row(jnp.pad(b_router, ((0, 0), (0, ROUTER_LANES - N_EXPERTS)))),
        w_gate_up=w_gate_up, b_gate_up=b_gate_up[:, :, None, :],
        w_down=w_down, b_down=b_down[:, :, None, :],
        final_norm_g=final_norm_g.reshape(1, -1),
    )


def kernel(x_prompt, x_sample, cache_k, cache_v, ln_mix_g, w_in, sgu_ln_g, sgu_ln_b, w_spatial, b_spatial,
           out_norm_a_g, out_norm_b_g, w_out, ln_ffn_g, w_router, b_router, w_gate_up, b_gate_up, w_down,
           b_down, final_norm_g):
    w = _prepare(ln_mix_g, w_in, sgu_ln_g, sgu_ln_b, w_spatial, b_spatial, out_norm_a_g, out_norm_b_g, w_out,
                 ln_ffn_g, w_router, b_router, w_gate_up, b_gate_up, w_down, b_down, final_norm_g)
    depth, dbsz, past = cache_k.shape[:3]
    ck = cache_k.reshape(depth, dbsz, past, A_WIDTH)
    cv = cache_v.reshape(depth, dbsz, past, A_WIDTH)
    xp, xs = x_prompt, x_sample
    kps, vps, kss, vss, zss = [], [], [], [], []
    for l in range(depth):
        xp, xs, kp, vp, ks, vs, zs = _layer(l, xp, xs, ck, cv, w)
        kps.append(kp)
        vps.append(vp)
        kss.append(ks)
        vss.append(vs)
        zss.append(zs)
    return (xp, xs, jnp.stack(kps), jnp.stack(vps), jnp.stack(kss), jnp.stack(vss), jnp.stack(zss))
```

```python
import functools
import math

import jax
import jax.numpy as jnp
from jax import lax
from jax.experimental import pallas as pl
from jax.experimental.pallas import tpu as pltpu

F32 = jnp.float32
BF16 = jnp.bfloat16

D_MODEL = 1024
HEAD_DIM = 64
N_A_HEADS = 8
A_WIDTH = N_A_HEADS * HEAD_DIM
N_B_GROUPS = 8
B_GROUP_DIM = 64
B_WIDTH = N_B_GROUPS * B_GROUP_DIM
CHUNK = 64
MLP_CHUNK = 128
SB_SCALE = HEAD_DIM ** -0.5
N_EXPERTS = 32
TOP_K = 4
D_FF = D_MODEL
SWIGLU_LIMIT = 7.0
SWIGLU_ALPHA = 1.702
NORM_EPS = 1e-5

LANES = 128
HEADS_PER_TILE = LANES // HEAD_DIM
N_HEAD_TILES = A_WIDTH // LANES
ROUTER_LANES = 128
VMEM_LIMIT = 56 * 1024 * 1024
TOKEN_TILE = 256
ATTN_Q_TILE = 256
ATTN_K_TILE = 256
MOE_BLOCK = 512
WEIGHT_CAST_ROWS = 128
FF_CHUNK = 256
RUN_ROWS = 16
DISPATCH_TILES = 4
DISPATCH_CHUNK = 512
ZERO_WEIGHT_LOG = -110.0


def _params(sem):
    return pltpu.CompilerParams(dimension_semantics=sem, vmem_limit_bytes=VMEM_LIMIT)


def _split_bf16(t):
    hi = t.astype(BF16)
    lo = (t - hi.astype(F32)).astype(BF16)
    return hi, lo


def _dot(a, b):
    return jnp.dot(a, b, preferred_element_type=F32)


def _dot_nt(a, b):
    return lax.dot_general(a, b, (((1,), (1,)), ((), ())), preferred_element_type=F32)


def _rms(x, g):
    return x * lax.rsqrt(jnp.mean(x * x, axis=-1, keepdims=True) + NORM_EPS) * g


def _inproj_kernel(x_ref, g_ref, w_ref, lng_ref, lnb_ref, gavg_ref, k_all_ref, v_all_ref,
                   q_ref, k_ref, v_ref, kb_ref, vb_ref, u_ref, zn_ref, znb_ref):
    del k_all_ref, v_all_ref
    hn = _rms(x_ref[...], g_ref[...]).astype(BF16)

    def proj(sec):
        return _dot(hn, w_ref[:, sec * A_WIDTH:(sec + 1) * A_WIDTH])

    q_ref[...] = (proj(0) * SB_SCALE).astype(BF16)
    k = proj(1)
    k_ref[...] = k
    kb_ref[...] = k.astype(BF16)
    v = proj(2)
    v_ref[...] = v
    vb_ref[...] = v.astype(BF16)
    u_ref[...] = jax.nn.gelu(proj(3)).astype(BF16)
    z = jax.nn.gelu(proj(4))
    gavg = gavg_ref[...]

    def group_mean(t):
        hi, lo = _split_bf16(t)
        return _dot(hi, gavg) + _dot(lo, gavg)

    zc = z - group_mean(z)
    var = group_mean(zc * zc)
    zn = zc * lax.rsqrt(var + NORM_EPS) * lng_ref[...] + lnb_ref[...]
    zn_ref[...] = zn
    znb_ref[...] = zn.astype(BF16)


def _inproj(x, g, w_bf, lng, lnb, gavg, k_all, v_all, layer):
    t = x.shape[0]
    tm = min(TOKEN_TILE, t)
    row = lambda w: pl.BlockSpec((tm, w), lambda i: (i, 0))
    full = lambda a: pl.BlockSpec(a.shape, lambda i: (0,) * a.ndim)
    sds = lambda w, dt: jax.ShapeDtypeStruct((t, w), dt)
    anywhere = pl.BlockSpec(memory_space=pl.ANY)
    layer_row = pl.BlockSpec((None, tm, A_WIDTH), lambda i: (layer, i, 0))
    all_sds = jax.ShapeDtypeStruct(k_all.shape, F32)
    return pl.pallas_call(
        _inproj_kernel,
        grid=(t // tm,),
        in_specs=[row(D_MODEL), full(g), full(w_bf), full(lng), full(lnb), full(gavg), anywhere, anywhere],
        out_specs=[row(A_WIDTH), layer_row, layer_row] + [row(A_WIDTH)] * 5,
        out_shape=[sds(A_WIDTH, BF16), all_sds, all_sds, sds(A_WIDTH, BF16),
                   sds(A_WIDTH, BF16), sds(B_WIDTH, BF16), sds(B_WIDTH, F32), sds(B_WIDTH, BF16)],
        input_output_aliases={6: 1, 7: 2},
        compiler_params=_params(("parallel",)),
        name="inproj",
    )(x, g, w_bf, lng, lnb, gavg, k_all, v_all)


def _log_terms(z):
    soft = jnp.log(1.0 + jnp.exp(-jnp.abs(z)))
    return jnp.minimum(z, 0.0) - soft, jnp.minimum(-z, 0.0) - soft


def _tri(n):
    j = lax.broadcasted_iota(jnp.int32, (n, n), 0)
    s = lax.broadcasted_iota(jnp.int32, (n, n), 1)
    return jnp.where(j > s, 1.0, 0.0).astype(BF16)


def _later_sum(log_keep, tri):
    hi, lo = _split_bf16(log_keep)
    return _dot(hi, tri) + _dot(lo, tri)


def _attn_kernel(q_ref, kd_ref, vd_ref, kp_ref, vp_ref, o_ref, *, n_past_fn, tk):
    tq = q_ref.shape[0]
    q = q_ref[...].astype(F32)
    lane = lax.broadcasted_iota(jnp.int32, (tq, LANES), 1)
    qs = [jnp.where((lane // HEAD_DIM) == h, q, 0.0).astype(BF16) for h in range(HEADS_PER_TILE)]

    kd = kd_ref[...].astype(BF16)
    vd = vd_ref[...].astype(BF16)
    t_pos = lax.broadcasted_iota(jnp.int32, (tq, tq), 0)
    s_pos = lax.broadcasted_iota(jnp.int32, (tq, tq), 1)
    visible = s_pos < t_pos
    tri_d = _tri(tq)
    accs, carries = [], []
    for qh in qs:
        log_beta, log_keep = _log_terms(_dot_nt(qh, kd))
        log_keep = jnp.where(visible, log_keep, 0.0)
        a = jnp.where(visible, jnp.exp(log_beta + _later_sum(log_keep, tri_d)), 0.0)
        accs.append(_dot(a.astype(BF16), vd))
        carries.append(jnp.sum(log_keep, axis=-1, keepdims=True))

    n_past = n_past_fn()
    tri_p = _tri(tk)

    def carry_max(carries):
        return functools.reduce(jnp.maximum, [jnp.max(c) for c in carries])

    def cond(state):
        j, _, _, cmax = state
        return jnp.logical_and(j < n_past, cmax > ZERO_WEIGHT_LOG)

    def past_block(j, accs, carries, live=None):
        start = pl.multiple_of(jnp.maximum(n_past - 1 - j, 0) * tk, tk)
        kp = kp_ref[pl.ds(start, tk), :].astype(BF16)
        vp = vp_ref[pl.ds(start, tk), :].astype(BF16)
        new_accs, new_carries = [], []
        for qh, acc, carry in zip(qs, accs, carries):
            log_beta, log_keep = _log_terms(_dot_nt(qh, kp))
            a = jnp.exp(log_beta + _later_sum(log_keep, tri_p) + carry)
            av = _dot(a.astype(BF16), vp)
            new_accs.append(acc + (av if live is None else jnp.where(live, av, 0.0)))
            new_carries.append(carry + jnp.sum(log_keep, axis=-1, keepdims=True))
        return tuple(new_accs), tuple(new_carries)

    def body(state):
        j, accs, carries, _ = state
        accs, carries = past_block(j, accs, carries)
        return j + 1, accs, carries, carry_max(carries)

    accs, carries = past_block(0, accs, carries, live=n_past > 0)
    _, accs, _, _ = lax.while_loop(cond, body, (jnp.int32(1), accs, carries, carry_max(carries)))
    out = accs[0]
    for h in range(1, HEADS_PER_TILE):
        out = jnp.where((lane // HEAD_DIM) == h, accs[h], out)
    o_ref[...] = out


def _attn_prompt(q_bf, k_bf, v_bf):
    b, s, _ = q_bf.shape
    tq = min(ATTN_Q_TILE, s)
    tk = tq
    blk = pl.BlockSpec((None, tq, LANES), lambda bi, hp, i: (bi, i, hp))
    seq = pl.BlockSpec((None, s, LANES), lambda bi, hp, i: (bi, 0, hp))
    kern = functools.partial(_attn_kernel, n_past_fn=lambda: pl.program_id(2) * (tq // tk), tk=tk)
    return pl.pallas_call(
        kern,
        grid=(b, N_HEAD_TILES, s // tq),
        in_specs=[blk, blk, blk, seq, seq],
        out_specs=blk,
        out_shape=jax.ShapeDtypeStruct((b, s, A_WIDTH), F32),
        compiler_params=_params(("parallel", "parallel", "arbitrary")),
        name="attn_prompt",
    )(q_bf, k_bf, v_bf, k_bf, v_bf)


def _attn_sample(q_bf, k_bf, v_bf, cache_k, cache_v, layer):
    b, n, _ = q_bf.shape
    p = cache_k.shape[2]
    tk = min(ATTN_K_TILE, p)
    blk = pl.BlockSpec((None, n, LANES), lambda bi, hp: (bi, 0, hp))
    past = pl.BlockSpec((None, None, p, LANES), lambda bi, hp: (layer, bi, 0, hp))
    kern = functools.partial(_attn_kernel, n_past_fn=lambda: p // tk, tk=tk)
    return pl.pallas_call(
        kern,
        grid=(b, N_HEAD_TILES),
        in_specs=[blk, blk, blk, past, past],
        out_specs=blk,
        out_shape=jax.ShapeDtypeStruct((b, n, A_WIDTH), F32),
        compiler_params=_params(("parallel", "parallel")),
        name="attn_sample",
    )(q_bf, k_bf, v_bf, cache_k, cache_v)


def _mixout_kernel(x_ref, oa_ref, u_ref, znb_ref, wsp_ref, bsp_ref, ga_ref, gb_ref, wout_ref,
                   lnf_ref, wrh_ref, wrl_ref, br_ref, x1_ref, h2_ref, lg_ref, mix_ref):
    tm = x_ref.shape[0]
    chunk = wsp_ref.shape[1]
    group = lax.broadcasted_iota(jnp.int32, (chunk, B_WIDTH), 1) // B_GROUP_DIM
    for c in range(tm // chunk):
        zc = znb_ref[c * chunk:(c + 1) * chunk, :]
        m = jnp.zeros((chunk, B_WIDTH), F32)
        for g in range(N_B_GROUPS):
            m = jnp.where(group == g, _dot(wsp_ref[g], zc), m)
        mix_ref[c * chunk:(c + 1) * chunk, :] = m + bsp_ref[...]
    ob = u_ref[...].astype(F32) * mix_ref[...]
    ya = _rms(oa_ref[...], ga_ref[...]).astype(BF16)
    yb = _rms(ob, gb_ref[...]).astype(BF16)
    x1 = x_ref[...] + _dot(ya, wout_ref[:A_WIDTH, :]) + _dot(yb, wout_ref[A_WIDTH:, :])
    x1_ref[...] = x1
    h2 = _rms(x1, lnf_ref[...])
    hi, lo = _split_bf16(h2)
    h2_ref[...] = hi
    wrh = wrh_ref[...]
    lg_ref[...] = _dot(hi, wrh) + _dot(lo, wrh) + _dot(hi, wrl_ref[...]) + br_ref[...]


def _mixout(x, oa, u, znb, wsp, bsp, ga, gb, wout_bf, lnf, wrh, wrl, br):
    t = x.shape[0]
    tm = min(TOKEN_TILE, t)
    row = lambda w: pl.BlockSpec((tm, w), lambda i: (i, 0))
    full = lambda a: pl.BlockSpec(a.shape, lambda i: (0,) * a.ndim)
    return pl.pallas_call(
        _mixout_kernel,
        grid=(t // tm,),
        in_specs=[row(D_MODEL), row(A_WIDTH), row(B_WIDTH), row(B_WIDTH), full(wsp), full(bsp),
                  full(ga), full(gb), full(wout_bf), full(lnf), full(wrh), full(wrl), full(br)],
        out_specs=[row(D_MODEL), row(D_MODEL), row(ROUTER_LANES)],
        out_shape=[jax.ShapeDtypeStruct((t, D_MODEL), F32), jax.ShapeDtypeStruct((t, D_MODEL), BF16),
                   jax.ShapeDtypeStruct((t, ROUTER_LANES), F32)],
        scratch_shapes=[pltpu.VMEM((tm, B_WIDTH), F32)],
        compiler_params=_params(("parallel",)),
        name="mixout",
    )(x, oa, u, znb, wsp, bsp, ga, gb, wout_bf, lnf, wrh, wrl, br)


def _moe_kernel(bexp_ref, bval_ref, xs_ref, wgu_ref, bgu_ref, wd_ref, bd_ref, o_ref, wgu_bf, wd_bf):
    i = pl.program_id(0)
    valid = bval_ref[i] > 0
    new_expert = jnp.logical_or(i == 0, bexp_ref[i] != bexp_ref[jnp.maximum(i - 1, 0)])

    @pl.when(jnp.logical_and(valid, new_expert))
    def _():
        def cast_rows(r, _):
            rows = pl.ds(pl.multiple_of(r * WEIGHT_CAST_ROWS, WEIGHT_CAST_ROWS), WEIGHT_CAST_ROWS)
            wgu_bf[rows, :] = wgu_ref[rows, :].astype(BF16)
            wd_bf[rows, :] = wd_ref[rows, :].astype(BF16)
            return 0
        lax.fori_loop(0, D_MODEL // WEIGHT_CAST_ROWS, cast_rows, 0)

    @pl.when(valid)
    def _():
        xs = xs_ref[...]
        acc = None
        for c in range(D_FF // FF_CHUNK):
            lo, hi = c * FF_CHUNK, (c + 1) * FF_CHUNK
            gate = _dot(xs, wgu_bf[:, lo:hi]) + bgu_ref[:, lo:hi]
            lin = _dot(xs, wgu_bf[:, D_FF + lo:D_FF + hi]) + bgu_ref[:, D_FF + lo:D_FF + hi]
            gate = jnp.minimum(gate, SWIGLU_LIMIT)
            lin = jnp.clip(lin, -SWIGLU_LIMIT, SWIGLU_LIMIT)
            act = gate * jax.nn.sigmoid(SWIGLU_ALPHA * gate) * (lin + 1.0)
            part = _dot(act.astype(BF16), wd_bf[lo:hi, :])
            acc = part if acc is None else acc + part
        o_ref[...] = (acc + bd_ref[...]).astype(o_ref.dtype)

    @pl.when(bval_ref[i] == 0)
    def _():
        o_ref[...] = jnp.zeros_like(o_ref)


def _moe(block_exp, block_valid, xs, wgu, bgu, wd, bd, layer):
    n_slots = xs.shape[0]
    bm = MOE_BLOCK
    grid_spec = pltpu.PrefetchScalarGridSpec(
        num_scalar_prefetch=2,
        grid=(n_slots // bm,),
        in_specs=[
            pl.BlockSpec((bm, D_MODEL), lambda i, be, bv: (i, 0)),
            pl.BlockSpec((None, None, D_MODEL, 2 * D_FF), lambda i, be, bv: (layer, be[i], 0, 0)),
            pl.BlockSpec((None, None, 1, 2 * D_FF), lambda i, be, bv: (layer, be[i], 0, 0)),
            pl.BlockSpec((None, None, D_FF, D_MODEL), lambda i, be, bv: (layer, be[i], 0, 0)),
            pl.BlockSpec((None, None, 1, D_MODEL), lambda i, be, bv: (layer, be[i], 0, 0)),
        ],
        out_specs=pl.BlockSpec((bm, D_MODEL), lambda i, be, bv: (i, 0)),
        scratch_shapes=[pltpu.VMEM((D_MODEL, 2 * D_FF), BF16), pltpu.VMEM((D_FF, D_MODEL), BF16)],
    )
    return pl.pallas_call(
        _moe_kernel,
        grid_spec=grid_spec,
        out_shape=jax.ShapeDtypeStruct((n_slots, D_MODEL), BF16),
        compiler_params=_params(("arbitrary",)),
        name="moe",
    )(block_exp, block_valid, xs, wgu, bgu, wd, bd)


def _collect_kernel(src_ref, cnt_ref, dst_ref, x_ref, pos_ref, gates_ref, g_ref, rows_hbm, y_ref, buf, sem,
                    *, final, tile_off):
    i = pl.program_id(0)
    n = pl.num_programs(0)

    def run_copies(tile, slot, wait):
        def per_expert(e, _):
            idx = (tile + tile_off) * N_EXPERTS + e
            src, dst = src_ref[idx], dst_ref[idx]

            def per_piece(c, _):
                cp = pltpu.make_async_copy(
                    rows_hbm.at[pl.ds(pl.multiple_of((src + c) * RUN_ROWS, RUN_ROWS), RUN_ROWS), :],
                    buf.at[slot, pl.ds(pl.multiple_of((dst + c) * RUN_ROWS, RUN_ROWS), RUN_ROWS), :],
                    sem.at[slot])
                if wait:
                    cp.wait()
                else:
                    cp.start()
                return 0

            lax.fori_loop(0, cnt_ref[idx], per_piece, 0)
            return 0

        lax.fori_loop(0, N_EXPERTS, per_expert, 0)

    @pl.when(i == 0)
    def _():
        buf[...] = jnp.zeros_like(buf)
        run_copies(0, 0, wait=False)

    @pl.when(i + 1 < n)
    def _():
        run_copies(i + 1, (i + 1) % 2, wait=False)

    slot = i % 2
    run_copies(i, slot, wait=True)

    tm = x_ref.shape[0]
    n_buf = buf.shape[1]
    pos = pos_ref[...]
    gates = gates_ref[...]
    r = lax.broadcasted_iota(jnp.int32, (tm, n_buf), 1)
    sel = jnp.zeros((tm, n_buf), F32)
    for k in range(TOP_K):
        sel = jnp.where(r == pos[:, TOP_K + k:TOP_K + k + 1], gates[:, k:k + 1], sel)
    hi, lo = _split_bf16(sel)
    rows = buf[slot]
    y = x_ref[...] + _dot(hi, rows) + _dot(lo, rows)
    y_ref[...] = _rms(y, g_ref[...]) if final else y


def _collect(x1, rows, runs, pos, gates, g, final, row_offset, tm):
    t = x1.shape[0]
    off = row_offset // tm
    n_buf = tm * TOP_K + 2 * N_EXPERTS * RUN_ROWS
    grid_spec = pltpu.PrefetchScalarGridSpec(
        num_scalar_prefetch=3,
        grid=(t // tm,),
        in_specs=[pl.BlockSpec((tm, D_MODEL), lambda i, *_: (i, 0)),
                  pl.BlockSpec((tm, ROUTER_LANES), lambda i, *_: (i + off, 0)),
                  pl.BlockSpec((tm, ROUTER_LANES), lambda i, *_: (i + off, 0)),
                  pl.BlockSpec(g.shape, lambda i, *_: (0, 0)),
                  pl.BlockSpec(memory_space=pl.ANY)],
        out_specs=pl.BlockSpec((tm, D_MODEL), lambda i, *_: (i, 0)),
        scratch_shapes=[pltpu.VMEM((2, n_buf, D_MODEL), BF16), pltpu.SemaphoreType.DMA((2,))],
    )
    return pl.pallas_call(
        functools.partial(_collect_kernel, final=final, tile_off=off),
        grid_spec=grid_spec,
        out_shape=jax.ShapeDtypeStruct((t, D_MODEL), F32),
        compiler_params=_params(("arbitrary",)),
        name="collect",
    )(*runs, x1, pos, gates, g, rows)


def _lane_prefix_sum(x):
    lane = lax.broadcasted_iota(jnp.int32, x.shape, 1)
    shift = 1
    while shift < ROUTER_LANES:
        x = x + jnp.where(lane >= shift, pltpu.roll(x, shift, axis=1), 0.0)
        shift *= 2
    return x


def _route_kernel(lg_ref, slots_ref, gates_ref, counts_ref, runs_ref, group_runs_ref,
                  base_ref, pstart_ref, group_start_ref, group_buf_ref, hist_ref):
    phase = pl.program_id(0)
    i = pl.program_id(1)
    group = i // DISPATCH_TILES
    first_in_group = i % DISPATCH_TILES == 0
    pieces_of = lambda rows: jnp.ceil(rows * (1.0 / RUN_ROWS))
    tm = lg_ref.shape[0]
    lane = lax.broadcasted_iota(jnp.int32, (tm, ROUTER_LANES), 1)
    lane_f = lane.astype(F32)
    x = jnp.where(lane < N_EXPERTS, lg_ref[...], -jnp.inf)
    vals, idxs = [], []
    sel = jnp.zeros((tm, ROUTER_LANES), F32)
    for _ in range(TOP_K):
        m = jnp.max(x, axis=-1, keepdims=True)
        idx = jnp.min(jnp.where(x == m, lane_f, float(ROUTER_LANES)), axis=-1, keepdims=True)
        hit = lane_f == idx
        x = jnp.where(hit, -jnp.inf, x)
        sel = jnp.where(hit, 1.0, sel)
        vals.append(m)
        idxs.append(idx)
    tile_counts = jnp.sum(sel, axis=0, keepdims=True)

    @pl.when(jnp.logical_and(phase == 0, i == 0))
    def _():
        base_ref[...] = jnp.zeros_like(base_ref)

    @pl.when(jnp.logical_and(phase == 0, first_in_group))
    def _():
        base_ref[...] = pieces_of(base_ref[...]) * RUN_ROWS
        hist_ref[group] = jnp.zeros_like(base_ref)

    @pl.when(phase == 0)
    def _():
        base_ref[...] = base_ref[...] + tile_counts
        hist_ref[group] = hist_ref[group] + tile_counts

    @pl.when(jnp.logical_and(phase == 1, i == 0))
    def _():
        counts = base_ref[...]
        counts_ref[...] = counts
        padded = jnp.ceil(counts * (1.0 / MOE_BLOCK)) * MOE_BLOCK
        pstart_ref[...] = _lane_prefix_sum(padded) - padded
        base_ref[...] = jnp.zeros_like(base_ref)

    @pl.when(jnp.logical_and(phase == 1, first_in_group))
    def _():
        base = pieces_of(base_ref[...]) * RUN_ROWS
        base_ref[...] = base
        group_start = base + pstart_ref[...]
        group_pieces = pieces_of(hist_ref[group])
        group_buf = (_lane_prefix_sum(group_pieces) - group_pieces) * RUN_ROWS
        group_start_ref[...] = group_start
        group_buf_ref[...] = group_buf
        sub = lax.broadcasted_iota(jnp.int32, base.shape, 0)
        table = jnp.where(sub == 0, group_start * (1.0 / RUN_ROWS),
                          jnp.where(sub == 1, group_pieces, group_buf * (1.0 / RUN_ROWS)))
        group_runs_ref[...] = table.astype(jnp.int32)

    @pl.when(phase == 1)
    def _():
        r = lax.broadcasted_iota(jnp.int32, (tm, tm), 0)
        c = lax.broadcasted_iota(jnp.int32, (tm, tm), 1)
        earlier = jnp.where(c < r, 1.0, 0.0).astype(BF16)
        run_start = base_ref[...] + pstart_ref[...]
        pos = _dot(earlier, sel.astype(BF16)) + run_start[0:1, :]
        aligned = jnp.floor(run_start * (1.0 / RUN_ROWS)) * RUN_ROWS
        pieces = jnp.where(tile_counts > 0.0, jnp.ceil((run_start - aligned + tile_counts) * (1.0 / RUN_ROWS)), 0.0)
        buf_base = (_lane_prefix_sum(pieces) - pieces) * RUN_ROWS
        to_buf = (buf_base - aligned)[0:1, :]
        to_group_buf = (group_buf_ref[...] - group_start_ref[...])[0:1, :]
        weights = [jnp.exp(v - vals[0]) for v in vals]
        total = functools.reduce(lambda a, b: a + b, weights)
        slots = jnp.zeros((tm, ROUTER_LANES), jnp.int32)
        gates = jnp.zeros((tm, ROUTER_LANES), F32)
        for k in range(TOP_K):
            mine = lane_f == idxs[k]
            slot_k = jnp.sum(jnp.where(mine, pos, 0.0), axis=-1, keepdims=True)
            buf_k = jnp.sum(jnp.where(mine, pos + to_buf, 0.0), axis=-1, keepdims=True)
            group_buf_k = jnp.sum(jnp.where(mine, pos + to_group_buf, 0.0), axis=-1, keepdims=True)
            slots = jnp.where(lane == k, slot_k.astype(jnp.int32), slots)
            slots = jnp.where(lane == TOP_K + k, buf_k.astype(jnp.int32), slots)
            slots = jnp.where(lane == 2 * TOP_K + k, group_buf_k.astype(jnp.int32), slots)
            gates = jnp.where(lane == k, weights[k] / total, gates)
        slots_ref[...] = slots
        gates_ref[...] = gates
        sub = lax.broadcasted_iota(jnp.int32, run_start.shape, 0)
        runs = jnp.where(sub == 0, aligned * (1.0 / RUN_ROWS),
                         jnp.where(sub == 1, pieces, buf_base * (1.0 / RUN_ROWS)))
        runs_ref[...] = runs.astype(jnp.int32)
        base_ref[...] = base_ref[...] + tile_counts


def _route(logits):
    t = logits.shape[0]
    tm = math.gcd(TOKEN_TILE, t)
    n_tiles = t // tm
    n_groups = -(-n_tiles // DISPATCH_TILES)
    out_blk = pl.BlockSpec((tm, ROUTER_LANES), lambda p, i: (i * p, 0))
    sub = 8
    vec = pltpu.VMEM((sub, ROUTER_LANES), F32)
    return pl.pallas_call(
        _route_kernel,
        grid=(2, n_tiles),
        in_specs=[pl.BlockSpec((tm, ROUTER_LANES), lambda p, i: (i, 0))],
        out_specs=[out_blk, out_blk, pl.BlockSpec((sub, ROUTER_LANES), lambda p, i: (0, 0)),
                   pl.BlockSpec((None, sub, ROUTER_LANES), lambda p, i: (i * p, 0, 0)),
                   pl.BlockSpec((None, sub, ROUTER_LANES), lambda p, i: (i // DISPATCH_TILES * p, 0, 0))],
        out_shape=[jax.ShapeDtypeStruct((t, ROUTER_LANES), jnp.int32),
                   jax.ShapeDtypeStruct((t, ROUTER_LANES), F32),
                   jax.ShapeDtypeStruct((sub, ROUTER_LANES), F32),
                   jax.ShapeDtypeStruct((n_tiles, sub, ROUTER_LANES), jnp.int32),
                   jax.ShapeDtypeStruct((n_groups, sub, ROUTER_LANES), jnp.int32)],
        scratch_shapes=[vec, vec, vec, vec, pltpu.VMEM((n_groups, sub, ROUTER_LANES), F32)],
        compiler_params=_params(("arbitrary", "arbitrary")),
        name="route",
    )(logits)


def _dispatch_kernel(dst_ref, cnt_ref, src_ref, tail_ref, tail_cnt_ref, valid_blocks_ref, hp_ref, hs_ref, pos_ref, xs_hbm,
                     buf, zeros, sem, tail_sem, *, n_tokens, n_prompt_groups):
    g = pl.program_id(0)
    n = pl.num_programs(0)
    tg = hp_ref.shape[0]
    n_buf = buf.shape[1]
    slot = g % 2

    def tail_copies(wait):
        def go(cp):
            if wait:
                cp.wait()
            else:
                cp.start()
            return 0

        def per_expert(e, _):
            def per_piece(c, _):
                rows = pl.ds(pl.multiple_of((tail_ref[e] + c) * RUN_ROWS, RUN_ROWS), RUN_ROWS)
                return go(pltpu.make_async_copy(zeros.at[pl.ds(0, RUN_ROWS), :], xs_hbm.at[rows, :], tail_sem))

            lax.fori_loop(0, tail_cnt_ref[e], per_piece, 0)
            return 0

        def per_block(b, _):
            rows = pl.ds(pl.multiple_of(b * MOE_BLOCK, MOE_BLOCK), MOE_BLOCK)
            return go(pltpu.make_async_copy(zeros, xs_hbm.at[rows, :], tail_sem))

        lax.fori_loop(0, N_EXPERTS, per_expert, 0)
        lax.fori_loop(valid_blocks_ref[0], xs_hbm.shape[0] // MOE_BLOCK, per_block, 0)

    @pl.when(g == 0)
    def _():
        zeros[...] = jnp.zeros_like(zeros)
        tail_copies(wait=False)

    def run_copies(grp, slot, wait):
        def per_expert(e, _):
            idx = grp * N_EXPERTS + e
            dst, src = dst_ref[idx], src_ref[idx]

            def per_piece(c, _):
                cp = pltpu.make_async_copy(
                    buf.at[slot, pl.ds(pl.multiple_of((src + c) * RUN_ROWS, RUN_ROWS), RUN_ROWS), :],
                    xs_hbm.at[pl.ds(pl.multiple_of((dst + c) * RUN_ROWS, RUN_ROWS), RUN_ROWS), :],
                    sem.at[slot])
                if wait:
                    cp.wait()
                else:
                    cp.start()
                return 0

            lax.fori_loop(0, cnt_ref[idx], per_piece, 0)
            return 0

        lax.fori_loop(0, N_EXPERTS, per_expert, 0)

    @pl.when(g >= 2)
    def _():
        run_copies(g - 2, slot, wait=True)

    pos_t = pos_ref[...].T
    tok = g * tg + lax.broadcasted_iota(jnp.int32, (1, tg), 1)
    rows_of = [jnp.where(tok < n_tokens, pos_t[2 * TOP_K + k:2 * TOP_K + k + 1, :], -1) for k in range(TOP_K)]
    h = jnp.where(g < n_prompt_groups, hp_ref[...], hs_ref[...])
    for c in range(n_buf // DISPATCH_CHUNK):
        r = c * DISPATCH_CHUNK + lax.broadcasted_iota(jnp.int32, (DISPATCH_CHUNK, tg), 0)
        sel = jnp.zeros((DISPATCH_CHUNK, tg), F32)
        for k in range(TOP_K):
            sel = jnp.where(r == rows_of[k], 1.0, sel)
        buf[slot, c * DISPATCH_CHUNK:(c + 1) * DISPATCH_CHUNK, :] = _dot(sel.astype(BF16), h).astype(BF16)

    run_copies(g, slot, wait=False)

    @pl.when(g == 0)
    def _():
        tail_copies(wait=True)

    @pl.when(jnp.logical_and(g == n - 1, g >= 1))
    def _():
        run_copies(g - 1, 1 - slot, wait=True)

    @pl.when(g == n - 1)
    def _():
        run_copies(g, slot, wait=True)


def _dispatch(h_prompt, h_sample, pos, group_tables, tail_tables, n_slots, tg):
    tp, ts = h_prompt.shape[0], h_sample.shape[0]
    assert tp % tg == 0 and ts <= tg
    n_prompt_groups = tp // tg
    h_sample = jnp.pad(h_sample, ((0, tg - ts), (0, 0)))
    n_buf = tg * TOP_K + N_EXPERTS * RUN_ROWS
    grid_spec = pltpu.PrefetchScalarGridSpec(
        num_scalar_prefetch=6,
        grid=(n_prompt_groups + 1,),
        in_specs=[pl.BlockSpec((tg, D_MODEL), lambda g, *_: (jnp.minimum(g, n_prompt_groups - 1), 0)),
                  pl.BlockSpec((tg, D_MODEL), lambda g, *_: (0, 0)),
                  pl.BlockSpec((tg, ROUTER_LANES), lambda g, *_: (g, 0))],
        out_specs=pl.BlockSpec(memory_space=pl.ANY),
        scratch_shapes=[pltpu.VMEM((2, n_buf, D_MODEL), BF16), pltpu.VMEM((MOE_BLOCK, D_MODEL), BF16),
                        pltpu.SemaphoreType.DMA((2,)), pltpu.SemaphoreType.DMA(())],
    )
    return pl.pallas_call(
        functools.partial(_dispatch_kernel, n_tokens=tp + ts, n_prompt_groups=n_prompt_groups),
        grid_spec=grid_spec,
        out_shape=jax.ShapeDtypeStruct((n_slots, D_MODEL), BF16),
        compiler_params=_params(("arbitrary",)),
        name="dispatch",
    )(*group_tables, *tail_tables, h_prompt, h_sample, pos)


def _block_tables(counts, n_blocks):
    padded = (counts + MOE_BLOCK - 1) // MOE_BLOCK * MOE_BLOCK
    pend = jnp.cumsum(padded)
    bstart = jnp.arange(n_blocks, dtype=jnp.int32) * MOE_BLOCK
    block_exp = jnp.minimum(jnp.sum((pend[None, :] <= bstart[:, None]).astype(jnp.int32), axis=1), N_EXPERTS - 1)
    block_valid = (bstart < pend[-1]).astype(jnp.int32)
    used = (counts + RUN_ROWS - 1) // RUN_ROWS
    tail_tables = ((pend - padded) // RUN_ROWS + used, padded // RUN_ROWS - used, pend[-1:] // MOE_BLOCK)
    return block_exp, block_valid, tail_tables


def _layer(l, xp, xs, kv_prompt, kv_sample, cache_k, cache_v, w):
    bsz, seq, _ = xp.shape
    dbsz, dseq, _ = xs.shape
    tp, ts = bsz * seq, dbsz * dseq

    def dense(x2, kv, n_stream, n_seq, chunk, attn_fn):
        q, k_all, v_all, kb, vb, u, zn, znb = _inproj(x2, w['ln_mix_g'][l], w['w_in'][l], w['sgu_ln_g'][l],
                                                      w['sgu_ln_b'][l], w['gavg'], kv[0], kv[1], l)
        r3 = lambda a: a.reshape(n_stream, n_seq, A_WIDTH)
        oa = attn_fn(r3(q), r3(kb), r3(vb)).reshape(n_stream * n_seq, A_WIDTH)
        wsp = w['w_spatial'][l][:, :chunk, :chunk]
        bsp = w['b_spatial'][l][:chunk]
        x1, h2, lg = _mixout(x2, oa, u, znb, wsp, bsp, w['out_norm_a_g'][l], w['out_norm_b_g'][l],
                             w['w_out'][l], w['ln_ffn_g'][l], w['wr_hi'][l], w['wr_lo'][l], w['b_router'][l])
        return x1, h2, lg, (k_all, v_all), zn

    x1p, h2p, lgp, kv_prompt, _ = dense(xp.reshape(tp, D_MODEL), kv_prompt, bsz, seq, MLP_CHUNK, _attn_prompt)
    x1s, h2s, lgs, kv_sample, zs = dense(xs.reshape(ts, D_MODEL), kv_sample, dbsz, dseq, dseq,
                                         lambda q, k, v: _attn_sample(q, k, v, cache_k, cache_v, l))

    t = tp + ts
    slots, gates, counts, runs, group_runs = _route(jnp.concatenate([lgp, lgs], axis=0))
    route_tile = t // runs.shape[0]
    n_groups = group_runs.shape[0]
    tg = route_tile * DISPATCH_TILES
    tables = lambda a: tuple(a[:, j, :N_EXPERTS].reshape(-1) for j in range(3))
    run_tables, group_tables = tables(runs), tables(group_runs)
    max_rows = t * TOP_K + n_groups * N_EXPERTS * (RUN_ROWS - 1)
    n_blocks = -(-max_rows // MOE_BLOCK) + N_EXPERTS + 1
    block_exp, block_valid, tail_tables = _block_tables(counts[0, :N_EXPERTS].astype(jnp.int32), n_blocks)
    x_sorted = _dispatch(h2p, h2s, slots, group_tables, tail_tables, n_blocks * MOE_BLOCK, tg)
    out = _moe(block_exp, block_valid, x_sorted, w['w_gate_up'], w['b_gate_up'], w['w_down'], w['b_down'], l)
    final = l == w['depth'] - 1
    yp = _collect(x1p, out, run_tables, slots, gates, w['final_norm_g'], final, 0, route_tile)
    ys = _collect(x1s, out, run_tables, slots, gates, w['final_norm_g'], final, tp, route_tile)
    return (yp.reshape(bsz, seq, D_MODEL), ys.reshape(dbsz, dseq, D_MODEL), kv_prompt, kv_sample,
            zs.reshape(dbsz, dseq, N_B_GROUPS, B_GROUP_DIM))


def _prepare(ln_mix_g, w_in, sgu_ln_g, sgu_ln_b, w_spatial, b_spatial, out_norm_a_g, out_norm_b_g, w_out,
             ln_ffn_g, w_router, b_router, w_gate_up, b_gate_up, w_down, b_down, final_norm_g):
    depth = w_in.shape[0]
    row = lambda a: a.reshape(depth, 1, -1)
    blk = jnp.arange(MLP_CHUNK, dtype=jnp.int32) // CHUNK
    mask = blk[None, :] <= blk[:, None]
    grp = jnp.arange(B_WIDTH, dtype=jnp.int32) // B_GROUP_DIM
    gavg = jnp.where(grp[:, None] == grp[None, :], 1.0 / B_GROUP_DIM, 0.0).astype(BF16)
    wr = jnp.pad(w_router, ((0, 0), (0, 0), (0, ROUTER_LANES - N_EXPERTS)))
    wr_hi = wr.astype(BF16)
    wr_lo = (wr - wr_hi.astype(F32)).astype(BF16)
    return dict(
        depth=depth,
        ln_mix_g=row(ln_mix_g), w_in=w_in.astype(BF16), sgu_ln_g=row(sgu_ln_g), sgu_ln_b=row(sgu_ln_b),
        gavg=gavg,
        w_spatial=jnp.where(mask[None, None], w_spatial, 0.0).astype(BF16),
        b_spatial=jnp.repeat(jnp.swapaxes(b_spatial, 1, 2), B_GROUP_DIM, axis=2),
        out_norm_a_g=row(out_norm_a_g), out_norm_b_g=row(out_norm_b_g), w_out=w_out.astype(BF16),
        ln_ffn_g=row(ln_ffn_g), wr_hi=wr_hi, wr_lo=wr_lo,
        b_router=row(jnp.pad(b_router, ((0, 0), (0, ROUTER_LANES - N_EXPERTS)))),
        w_gate_up=w_gate_up, b_gate_up=b_gate_up[:, :, None, :],
        w_down=w_down, b_down=b_down[:, :, None, :],
        final_norm_g=final_norm_g.reshape(1, -1),
    )


def kernel(x_prompt, x_sample, cache_k, cache_v, ln_mix_g, w_in, sgu_ln_g, sgu_ln_b, w_spatial, b_spatial,
           out_norm_a_g, out_norm_b_g, w_out, ln_ffn_g, w_router, b_router, w_gate_up, b_gate_up, w_down,
           b_down, final_norm_g):
    w = _prepare(ln_mix_g, w_in, sgu_ln_g, sgu_ln_b, w_spatial, b_spatial, out_norm_a_g, out_norm_b_g, w_out,
                 ln_ffn_g, w_router, b_router, w_gate_up, b_gate_up, w_down, b_down, final_norm_g)
    depth, dbsz, past = cache_k.shape[:3]
    ck = cache_k.reshape(depth, dbsz, past, A_WIDTH)
    cv = cache_v.reshape(depth, dbsz, past, A_WIDTH)
    xp, xs = x_prompt, x_sample
    bsz, seq, _ = xp.shape
    dseq = xs.shape[1]
    blank = lambda t: jnp.zeros((depth, t, A_WIDTH), F32)
    kv_prompt = (blank(bsz * seq), blank(bsz * seq))
    kv_sample = (blank(dbsz * dseq), blank(dbsz * dseq))
    zss = []
    for l in range(depth):
        xp, xs, kv_prompt, kv_sample, zs = _layer(l, xp, xs, kv_prompt, kv_sample, ck, cv, w)
        zss.append(zs)
    heads = lambda a, n_stream, n_seq: a.reshape(depth, n_stream, n_seq, N_A_HEADS, HEAD_DIM)
    return (xp, xs, heads(kv_prompt[0], bsz, seq), heads(kv_prompt[1], bsz, seq),
            heads(kv_sample[0], dbsz, dseq), heads(kv_sample[1], dbsz, dseq), jnp.stack(zss))
```

```python
import functools
import math

import jax
import jax.numpy as jnp
from jax import lax
from jax.experimental import pallas as pl
from jax.experimental.pallas import tpu as pltpu

F32 = jnp.float32
BF16 = jnp.bfloat16

D_MODEL = 1024
HEAD_DIM = 64
N_A_HEADS = 8
A_WIDTH = N_A_HEADS * HEAD_DIM
N_B_GROUPS = 8
B_GROUP_DIM = 64
B_WIDTH = N_B_GROUPS * B_GROUP_DIM
CHUNK = 64
MLP_CHUNK = 128
SB_SCALE = HEAD_DIM ** -0.5
N_EXPERTS = 32
TOP_K = 4
D_FF = D_MODEL
SWIGLU_LIMIT = 7.0
SWIGLU_ALPHA = 1.702
NORM_EPS = 1e-5

LANES = 128
HEADS_PER_TILE = LANES // HEAD_DIM
N_HEAD_TILES = A_WIDTH // LANES
ROUTER_LANES = 128
VMEM_LIMIT = 56 * 1024 * 1024
TOKEN_TILE = 256
DENSE_TILE = 512
ATTN_Q_TILE = 256
ATTN_K_TILE = 256
MOE_BLOCK = 512
WEIGHT_CAST_ROWS = 128
FF_CHUNK = 256
RUN_ROWS = 16
COLLECT_CHUNK = 256
DISPATCH_TILES = 4
DISPATCH_CHUNK = 512
ZERO_WEIGHT_LOG = -110.0


def _params(sem):
    return pltpu.CompilerParams(dimension_semantics=sem, vmem_limit_bytes=VMEM_LIMIT)


def _split_bf16(t):
    hi = t.astype(BF16)
    lo = (t - hi.astype(F32)).astype(BF16)
    return hi, lo


def _dot(a, b):
    return jnp.dot(a, b, preferred_element_type=F32)


def _dot_nt(a, b):
    return lax.dot_general(a, b, (((1,), (1,)), ((), ())), preferred_element_type=F32)


def _rms(x, g):
    return x * lax.rsqrt(jnp.mean(x * x, axis=-1, keepdims=True) + NORM_EPS) * g


def _inproj_kernel(*refs, first):
    x_ref, g_ref, w_ref, lng_ref, lnb_ref, gavg_ref = refs[:6]
    q_ref, k_ref, v_ref, kb_ref, vb_ref, u_ref, zn_ref, znb_ref = refs[-8:]
    hn = _rms(x_ref[...], g_ref[...]).astype(BF16)

    def proj(sec):
        return _dot(hn, w_ref[:, sec * A_WIDTH:(sec + 1) * A_WIDTH])

    def put(all_ref, val):
        if first:
            all_ref[0] = val
            if all_ref.shape[0] > 1:
                all_ref[1:] = jnp.zeros((all_ref.shape[0] - 1,) + val.shape, F32)
        else:
            all_ref[...] = val

    q_ref[...] = (proj(0) * SB_SCALE).astype(BF16)
    k = proj(1)
    put(k_ref, k)
    kb_ref[...] = k.astype(BF16)
    v = proj(2)
    put(v_ref, v)
    vb_ref[...] = v.astype(BF16)
    u_ref[...] = jax.nn.gelu(proj(3)).astype(BF16)
    z = jax.nn.gelu(proj(4))
    gavg = gavg_ref[...]

    def group_mean(t):
        hi, lo = _split_bf16(t)
        return _dot(hi, gavg) + _dot(lo, gavg)

    zc = z - group_mean(z)
    var = group_mean(zc * zc)
    zn = zc * lax.rsqrt(var + NORM_EPS) * lng_ref[...] + lnb_ref[...]
    zn_ref[...] = zn
    znb_ref[...] = zn.astype(BF16)


def _inproj(x, g, w_bf, lng, lnb, gavg, kv_all, layer, depth):
    t = x.shape[0]
    tm = min(DENSE_TILE, t)
    row = lambda w: pl.BlockSpec((tm, w), lambda i: (i, 0))
    full = lambda a: pl.BlockSpec(a.shape, lambda i: (0,) * a.ndim)
    sds = lambda w, dt: jax.ShapeDtypeStruct((t, w), dt)
    first = kv_all is None
    if first:
        extra_specs, extra_args, aliases = [], (), {}
        layer_rows = pl.BlockSpec((depth, tm, A_WIDTH), lambda i: (0, i, 0))
    else:
        extra_specs, extra_args, aliases = [pl.BlockSpec(memory_space=pl.ANY)] * 2, tuple(kv_all), {6: 1, 7: 2}
        layer_rows = pl.BlockSpec((None, tm, A_WIDTH), lambda i: (layer, i, 0))
    all_sds = jax.ShapeDtypeStruct((depth, t, A_WIDTH), F32)
    return pl.pallas_call(
        functools.partial(_inproj_kernel, first=first),
        grid=(t // tm,),
        in_specs=[row(D_MODEL), full(g), full(w_bf), full(lng), full(lnb), full(gavg)] + extra_specs,
        out_specs=[row(A_WIDTH), layer_rows, layer_rows] + [row(A_WIDTH)] * 5,
        out_shape=[sds(A_WIDTH, BF16), all_sds, all_sds, sds(A_WIDTH, BF16),
                   sds(A_WIDTH, BF16), sds(B_WIDTH, BF16), sds(B_WIDTH, F32), sds(B_WIDTH, BF16)],
        input_output_aliases=aliases,
        compiler_params=_params(("parallel",)),
        name="inproj",
    )(x, g, w_bf, lng, lnb, gavg, *extra_args)


def _log_terms(z):
    soft = jnp.log(1.0 + jnp.exp(-jnp.abs(z)))
    return jnp.minimum(z, 0.0) - soft, jnp.minimum(-z, 0.0) - soft


def _tri(n):
    j = lax.broadcasted_iota(jnp.int32, (n, n), 0)
    s = lax.broadcasted_iota(jnp.int32, (n, n), 1)
    return jnp.where(j > s, 1.0, 0.0).astype(BF16)


def _later_sum(log_keep, tri):
    hi, lo = _split_bf16(log_keep)
    return _dot(hi, tri) + _dot(lo, tri)


def _attn_kernel(q_ref, kd_ref, vd_ref, kp_ref, vp_ref, o_ref, *, n_past_fn, tk):
    tq = q_ref.shape[0]
    q = q_ref[...].astype(F32)
    lane = lax.broadcasted_iota(jnp.int32, (tq, LANES), 1)
    qs = [jnp.where((lane // HEAD_DIM) == h, q, 0.0).astype(BF16) for h in range(HEADS_PER_TILE)]

    kd = kd_ref[...].astype(BF16)
    vd = vd_ref[...].astype(BF16)
    t_pos = lax.broadcasted_iota(jnp.int32, (tq, tq), 0)
    s_pos = lax.broadcasted_iota(jnp.int32, (tq, tq), 1)
    visible = s_pos < t_pos
    tri_d = _tri(tq)
    accs, carries = [], []
    for qh in qs:
        log_beta, log_keep = _log_terms(_dot_nt(qh, kd))
        log_keep = jnp.where(visible, log_keep, 0.0)
        a = jnp.where(visible, jnp.exp(log_beta + _later_sum(log_keep, tri_d)), 0.0)
        accs.append(_dot(a.astype(BF16), vd))
        carries.append(jnp.sum(log_keep, axis=-1, keepdims=True))

    n_past = n_past_fn()
    tri_p = _tri(tk)

    def carry_max(carries):
        return functools.reduce(jnp.maximum, [jnp.max(c) for c in carries])

    def cond(state):
        j, _, _, cmax = state
        return jnp.logical_and(j < n_past, cmax > ZERO_WEIGHT_LOG)

    def past_block(j, accs, carries, live=None):
        start = pl.multiple_of(jnp.maximum(n_past - 1 - j, 0) * tk, tk)
        kp = kp_ref[pl.ds(start, tk), :].astype(BF16)
        vp = vp_ref[pl.ds(start, tk), :].astype(BF16)
        new_accs, new_carries = [], []
        for qh, acc, carry in zip(qs, accs, carries):
            log_beta, log_keep = _log_terms(_dot_nt(qh, kp))
            a = jnp.exp(log_beta + _later_sum(log_keep, tri_p) + carry)
            av = _dot(a.astype(BF16), vp)
            new_accs.append(acc + (av if live is None else jnp.where(live, av, 0.0)))
            new_carries.append(carry + jnp.sum(log_keep, axis=-1, keepdims=True))
        return tuple(new_accs), tuple(new_carries)

    def body(state):
        j, accs, carries, _ = state
        accs, carries = past_block(j, accs, carries)
        return j + 1, accs, carries, carry_max(carries)

    accs, carries = past_block(0, accs, carries, live=n_past > 0)
    _, accs, _, _ = lax.while_loop(cond, body, (jnp.int32(1), accs, carries, carry_max(carries)))
    out = accs[0]
    for h in range(1, HEADS_PER_TILE):
        out = jnp.where((lane // HEAD_DIM) == h, accs[h], out)
    o_ref[...] = out


def _attn_prompt(q_bf, k_bf, v_bf):
    b, s, _ = q_bf.shape
    tq = min(ATTN_Q_TILE, s)
    tk = tq
    blk = pl.BlockSpec((None, tq, LANES), lambda bi, hp, i: (bi, i, hp))
    seq = pl.BlockSpec((None, s, LANES), lambda bi, hp, i: (bi, 0, hp))
    kern = functools.partial(_attn_kernel, n_past_fn=lambda: pl.program_id(2) * (tq // tk), tk=tk)
    return pl.pallas_call(
        kern,
        grid=(b, N_HEAD_TILES, s // tq),
        in_specs=[blk, blk, blk, seq, seq],
        out_specs=blk,
        out_shape=jax.ShapeDtypeStruct((b, s, A_WIDTH), F32),
        compiler_params=_params(("parallel", "parallel", "arbitrary")),
        name="attn_prompt",
    )(q_bf, k_bf, v_bf, k_bf, v_bf)


def _attn_sample(q_bf, k_bf, v_bf, cache_k, cache_v, layer):
    b, n, _ = q_bf.shape
    p = cache_k.shape[2]
    tk = min(ATTN_K_TILE, p)
    blk = pl.BlockSpec((None, n, LANES), lambda bi, hp: (bi, 0, hp))
    past = pl.BlockSpec((None, None, p, LANES), lambda bi, hp: (layer, bi, 0, hp))
    kern = functools.partial(_attn_kernel, n_past_fn=lambda: p // tk, tk=tk)
    return pl.pallas_call(
        kern,
        grid=(b, N_HEAD_TILES),
        in_specs=[blk, blk, blk, past, past],
        out_specs=blk,
        out_shape=jax.ShapeDtypeStruct((b, n, A_WIDTH), F32),
        compiler_params=_params(("parallel", "parallel")),
        name="attn_sample",
    )(q_bf, k_bf, v_bf, cache_k, cache_v)


def _mixout_kernel(x_ref, oa_ref, u_ref, znb_ref, wsp_ref, bsp_ref, ga_ref, gb_ref, wout_ref,
                   lnf_ref, wrh_ref, wrl_ref, br_ref, x1_ref, h2_ref, lg_ref, mix_ref):
    tm = x_ref.shape[0]
    chunk = wsp_ref.shape[1]
    group = lax.broadcasted_iota(jnp.int32, (chunk, B_WIDTH), 1) // B_GROUP_DIM
    for c in range(tm // chunk):
        zc = znb_ref[c * chunk:(c + 1) * chunk, :]
        m = jnp.zeros((chunk, B_WIDTH), F32)
        for g in range(N_B_GROUPS):
            m = jnp.where(group == g, _dot(wsp_ref[g], zc), m)
        mix_ref[c * chunk:(c + 1) * chunk, :] = m + bsp_ref[...]
    ob = u_ref[...].astype(F32) * mix_ref[...]
    ya = _rms(oa_ref[...], ga_ref[...]).astype(BF16)
    yb = _rms(ob, gb_ref[...]).astype(BF16)
    x1 = x_ref[...] + _dot(ya, wout_ref[:A_WIDTH, :]) + _dot(yb, wout_ref[A_WIDTH:, :])
    x1_ref[...] = x1
    h2 = _rms(x1, lnf_ref[...])
    hi, lo = _split_bf16(h2)
    h2_ref[...] = hi
    wrh = wrh_ref[...]
    lg_ref[...] = _dot(hi, wrh) + _dot(lo, wrh) + _dot(hi, wrl_ref[...]) + br_ref[...]


def _mixout(x, oa, u, znb, wsp, bsp, ga, gb, wout_bf, lnf, wrh, wrl, br):
    t = x.shape[0]
    tm = min(DENSE_TILE, t)
    row = lambda w: pl.BlockSpec((tm, w), lambda i: (i, 0))
    full = lambda a: pl.BlockSpec(a.shape, lambda i: (0,) * a.ndim)
    return pl.pallas_call(
        _mixout_kernel,
        grid=(t // tm,),
        in_specs=[row(D_MODEL), row(A_WIDTH), row(B_WIDTH), row(B_WIDTH), full(wsp), full(bsp),
                  full(ga), full(gb), full(wout_bf), full(lnf), full(wrh), full(wrl), full(br)],
        out_specs=[row(D_MODEL), row(D_MODEL), row(ROUTER_LANES)],
        out_shape=[jax.ShapeDtypeStruct((t, D_MODEL), F32), jax.ShapeDtypeStruct((t, D_MODEL), BF16),
                   jax.ShapeDtypeStruct((t, ROUTER_LANES), F32)],
        scratch_shapes=[pltpu.VMEM((tm, B_WIDTH), F32)],
        compiler_params=_params(("parallel",)),
        name="mixout",
    )(x, oa, u, znb, wsp, bsp, ga, gb, wout_bf, lnf, wrh, wrl, br)


def _moe_kernel(bexp_ref, bval_ref, xs_ref, wgu_ref, bgu_ref, wd_ref, bd_ref, o_ref, wgu_bf, wd_bf):
    i = pl.program_id(0)
    valid = bval_ref[i] > 0
    new_expert = jnp.logical_or(i == 0, bexp_ref[i] != bexp_ref[jnp.maximum(i - 1, 0)])

    @pl.when(jnp.logical_and(valid, new_expert))
    def _():
        def cast_rows(r, _):
            rows = pl.ds(pl.multiple_of(r * WEIGHT_CAST_ROWS, WEIGHT_CAST_ROWS), WEIGHT_CAST_ROWS)
            for c in range(D_FF // FF_CHUNK):
                lo, hi = c * FF_CHUNK, (c + 1) * FF_CHUNK
                wgu_bf[rows, 2 * lo:2 * lo + FF_CHUNK] = wgu_ref[rows, lo:hi].astype(BF16)
                wgu_bf[rows, 2 * lo + FF_CHUNK:2 * hi] = wgu_ref[rows, D_FF + lo:D_FF + hi].astype(BF16)
            wd_bf[rows, :] = wd_ref[rows, :].astype(BF16)
            return 0
        lax.fori_loop(0, D_MODEL // WEIGHT_CAST_ROWS, cast_rows, 0)

    @pl.when(valid)
    def _():
        xs = xs_ref[...]
        acc = None
        for c in range(D_FF // FF_CHUNK):
            lo, hi = c * FF_CHUNK, (c + 1) * FF_CHUNK
            gate_lin = _dot(xs, wgu_bf[:, 2 * lo:2 * hi])
            gate = gate_lin[:, :FF_CHUNK] + bgu_ref[:, lo:hi]
            lin = gate_lin[:, FF_CHUNK:] + bgu_ref[:, D_FF + lo:D_FF + hi]
            gate = jnp.minimum(gate, SWIGLU_LIMIT)
            lin = jnp.clip(lin, -SWIGLU_LIMIT, SWIGLU_LIMIT)
            act = gate * jax.nn.sigmoid(SWIGLU_ALPHA * gate) * (lin + 1.0)
            part = _dot(act.astype(BF16), wd_bf[lo:hi, :])
            acc = part if acc is None else acc + part
        o_ref[...] = (acc + bd_ref[...]).astype(o_ref.dtype)

    @pl.when(bval_ref[i] == 0)
    def _():
        o_ref[...] = jnp.zeros_like(o_ref)


def _moe(block_exp, block_valid, xs, wgu, bgu, wd, bd, layer):
    n_slots = xs.shape[0]
    bm = MOE_BLOCK
    grid_spec = pltpu.PrefetchScalarGridSpec(
        num_scalar_prefetch=2,
        grid=(n_slots // bm,),
        in_specs=[
            pl.BlockSpec((bm, D_MODEL), lambda i, be, bv: (i, 0)),
            pl.BlockSpec((None, None, D_MODEL, 2 * D_FF), lambda i, be, bv: (layer, be[i], 0, 0)),
            pl.BlockSpec((None, None, 1, 2 * D_FF), lambda i, be, bv: (layer, be[i], 0, 0)),
            pl.BlockSpec((None, None, D_FF, D_MODEL), lambda i, be, bv: (layer, be[i], 0, 0)),
            pl.BlockSpec((None, None, 1, D_MODEL), lambda i, be, bv: (layer, be[i], 0, 0)),
        ],
        out_specs=pl.BlockSpec((bm, D_MODEL), lambda i, be, bv: (i, 0)),
        scratch_shapes=[pltpu.VMEM((D_MODEL, 2 * D_FF), BF16), pltpu.VMEM((D_FF, D_MODEL), BF16)],
    )
    return pl.pallas_call(
        _moe_kernel,
        grid_spec=grid_spec,
        out_shape=jax.ShapeDtypeStruct((n_slots, D_MODEL), BF16),
        compiler_params=_params(("arbitrary",)),
        name="moe",
    )(block_exp, block_valid, xs, wgu, bgu, wd, bd)


def _collect_kernel(src_ref, cnt_ref, dst_ref, x_ref, pos_ref, gates_ref, g_ref, rows_hbm, y_ref, buf, sem,
                    *, final, tile_off):
    i = pl.program_id(0)
    n = pl.num_programs(0)

    def run_copies(tile, slot, wait):
        def per_expert(e, _):
            idx = (tile + tile_off) * N_EXPERTS + e
            src, dst = src_ref[idx], dst_ref[idx]

            def per_piece(c, _):
                cp = pltpu.make_async_copy(
                    rows_hbm.at[pl.ds(pl.multiple_of((src + c) * RUN_ROWS, RUN_ROWS), RUN_ROWS), :],
                    buf.at[slot, pl.ds(pl.multiple_of((dst + c) * RUN_ROWS, RUN_ROWS), RUN_ROWS), :],
                    sem.at[slot])
                if wait:
                    cp.wait()
                else:
                    cp.start()
                return 0

            lax.fori_loop(0, cnt_ref[idx], per_piece, 0)
            return 0

        lax.fori_loop(0, N_EXPERTS, per_expert, 0)

    @pl.when(i == 0)
    def _():
        buf[...] = jnp.zeros_like(buf)
        run_copies(0, 0, wait=False)

    @pl.when(i + 1 < n)
    def _():
        run_copies(i + 1, (i + 1) % 2, wait=False)

    slot = i % 2
    run_copies(i, slot, wait=True)

    tm = x_ref.shape[0]
    pos = pos_ref[...]
    gates = gates_ref[...]
    buf_rows = [pos[:, TOP_K + k:TOP_K + k + 1] for k in range(TOP_K)]
    weights = [gates[:, k:k + 1] for k in range(TOP_K)]
    last = (i + tile_off) * N_EXPERTS + N_EXPERTS - 1
    used_rows = (dst_ref[last] + cnt_ref[last]) * RUN_ROWS
    y_ref[...] = x_ref[...]

    def chunk(c, _):
        start = pl.multiple_of(c * COLLECT_CHUNK, COLLECT_CHUNK)
        r = start + lax.broadcasted_iota(jnp.int32, (tm, COLLECT_CHUNK), 1)
        sel = jnp.zeros((tm, COLLECT_CHUNK), F32)
        for k in range(TOP_K):
            sel = jnp.where(r == buf_rows[k], weights[k], sel)
        hi, lo = _split_bf16(sel)
        rows = buf[slot, pl.ds(start, COLLECT_CHUNK), :]
        y_ref[...] = y_ref[...] + _dot(hi, rows) + _dot(lo, rows)
        return 0

    lax.fori_loop(0, (used_rows + COLLECT_CHUNK - 1) // COLLECT_CHUNK, chunk, 0)
    if final:
        y_ref[...] = _rms(y_ref[...], g_ref[...])


def _collect(x1, rows, runs, pos, gates, g, final, row_offset, tm):
    t = x1.shape[0]
    off = row_offset // tm
    n_buf = tm * TOP_K + 2 * N_EXPERTS * RUN_ROWS
    grid_spec = pltpu.PrefetchScalarGridSpec(
        num_scalar_prefetch=3,
        grid=(t // tm,),
        in_specs=[pl.BlockSpec((tm, D_MODEL), lambda i, *_: (i, 0)),
                  pl.BlockSpec((tm, ROUTER_LANES), lambda i, *_: (i + off, 0)),
                  pl.BlockSpec((tm, ROUTER_LANES), lambda i, *_: (i + off, 0)),
                  pl.BlockSpec(g.shape, lambda i, *_: (0, 0)),
                  pl.BlockSpec(memory_space=pl.ANY)],
        out_specs=pl.BlockSpec((tm, D_MODEL), lambda i, *_: (i, 0)),
        scratch_shapes=[pltpu.VMEM((2, n_buf, D_MODEL), BF16), pltpu.SemaphoreType.DMA((2,))],
    )
    return pl.pallas_call(
        functools.partial(_collect_kernel, final=final, tile_off=off),
        grid_spec=grid_spec,
        out_shape=jax.ShapeDtypeStruct((t, D_MODEL), F32),
        compiler_params=_params(("arbitrary",)),
        name="collect",
    )(*runs, x1, pos, gates, g, rows)


def _lane_prefix_sum(x):
    lane = lax.broadcasted_iota(jnp.int32, x.shape, 1)
    shift = 1
    while shift < ROUTER_LANES:
        x = x + jnp.where(lane >= shift, pltpu.roll(x, shift, axis=1), 0.0)
        shift *= 2
    return x


def _route_kernel(lg_ref, slots_ref, gates_ref, counts_ref, runs_ref, group_runs_ref,
                  base_ref, pstart_ref, group_start_ref, group_buf_ref, hist_ref):
    phase = pl.program_id(0)
    i = pl.program_id(1)
    group = i // DISPATCH_TILES
    first_in_group = i % DISPATCH_TILES == 0
    pieces_of = lambda rows: jnp.ceil(rows * (1.0 / RUN_ROWS))
    tm = lg_ref.shape[0]
    lane = lax.broadcasted_iota(jnp.int32, (tm, ROUTER_LANES), 1)
    lane_f = lane.astype(F32)
    x = jnp.where(lane < N_EXPERTS, lg_ref[...], -jnp.inf)
    vals, idxs = [], []
    sel = jnp.zeros((tm, ROUTER_LANES), F32)
    for _ in range(TOP_K):
        m = jnp.max(x, axis=-1, keepdims=True)
        idx = jnp.min(jnp.where(x == m, lane_f, float(ROUTER_LANES)), axis=-1, keepdims=True)
        hit = lane_f == idx
        x = jnp.where(hit, -jnp.inf, x)
        sel = jnp.where(hit, 1.0, sel)
        vals.append(m)
        idxs.append(idx)
    tile_counts = jnp.sum(sel, axis=0, keepdims=True)

    @pl.when(jnp.logical_and(phase == 0, i == 0))
    def _():
        base_ref[...] = jnp.zeros_like(base_ref)

    @pl.when(jnp.logical_and(phase == 0, first_in_group))
    def _():
        base_ref[...] = pieces_of(base_ref[...]) * RUN_ROWS
        hist_ref[group] = jnp.zeros_like(base_ref)

    @pl.when(phase == 0)
    def _():
        base_ref[...] = base_ref[...] + tile_counts
        hist_ref[group] = hist_ref[group] + tile_counts

    @pl.when(jnp.logical_and(phase == 1, i == 0))
    def _():
        counts = base_ref[...]
        counts_ref[...] = counts
        padded = jnp.ceil(counts * (1.0 / MOE_BLOCK)) * MOE_BLOCK
        pstart_ref[...] = _lane_prefix_sum(padded) - padded
        base_ref[...] = jnp.zeros_like(base_ref)

    @pl.when(jnp.logical_and(phase == 1, first_in_group))
    def _():
        base = pieces_of(base_ref[...]) * RUN_ROWS
        base_ref[...] = base
        group_start = base + pstart_ref[...]
        group_pieces = pieces_of(hist_ref[group])
        group_buf = (_lane_prefix_sum(group_pieces) - group_pieces) * RUN_ROWS
        group_start_ref[...] = group_start
        group_buf_ref[...] = group_buf
        sub = lax.broadcasted_iota(jnp.int32, base.shape, 0)
        table = jnp.where(sub == 0, group_start * (1.0 / RUN_ROWS),
                          jnp.where(sub == 1, group_pieces, group_buf * (1.0 / RUN_ROWS)))
        group_runs_ref[...] = table.astype(jnp.int32)

    @pl.when(phase == 1)
    def _():
        r = lax.broadcasted_iota(jnp.int32, (tm, tm), 0)
        c = lax.broadcasted_iota(jnp.int32, (tm, tm), 1)
        earlier = jnp.where(c < r, 1.0, 0.0).astype(BF16)
        run_start = base_ref[...] + pstart_ref[...]
        pos = _dot(earlier, sel.astype(BF16)) + run_start[0:1, :]
        aligned = jnp.floor(run_start * (1.0 / RUN_ROWS)) * RUN_ROWS
        pieces = jnp.where(tile_counts > 0.0, jnp.ceil((run_start - aligned + tile_counts) * (1.0 / RUN_ROWS)), 0.0)
        buf_base = (_lane_prefix_sum(pieces) - pieces) * RUN_ROWS
        to_buf = (buf_base - aligned)[0:1, :]
        to_group_buf = (group_buf_ref[...] - group_start_ref[...])[0:1, :]
        weights = [jnp.exp(v - vals[0]) for v in vals]
        total = functools.reduce(lambda a, b: a + b, weights)
        slots = jnp.zeros((tm, ROUTER_LANES), jnp.int32)
        gates = jnp.zeros((tm, ROUTER_LANES), F32)
        for k in range(TOP_K):
            mine = lane_f == idxs[k]
            slot_k = jnp.sum(jnp.where(mine, pos, 0.0), axis=-1, keepdims=True)
            buf_k = jnp.sum(jnp.where(mine, pos + to_buf, 0.0), axis=-1, keepdims=True)
            group_buf_k = jnp.sum(jnp.where(mine, pos + to_group_buf, 0.0), axis=-1, keepdims=True)
            slots = jnp.where(lane == k, slot_k.astype(jnp.int32), slots)
            slots = jnp.where(lane == TOP_K + k, buf_k.astype(jnp.int32), slots)
            slots = jnp.where(lane == 2 * TOP_K + k, group_buf_k.astype(jnp.int32), slots)
            gates = jnp.where(lane == k, weights[k] / total, gates)
        slots_ref[...] = slots
        gates_ref[...] = gates
        sub = lax.broadcasted_iota(jnp.int32, run_start.shape, 0)
        runs = jnp.where(sub == 0, aligned * (1.0 / RUN_ROWS),
                         jnp.where(sub == 1, pieces, buf_base * (1.0 / RUN_ROWS)))
        runs_ref[...] = runs.astype(jnp.int32)
        base_ref[...] = base_ref[...] + tile_counts


def _route(logits):
    t = logits.shape[0]
    tm = math.gcd(TOKEN_TILE, t)
    n_tiles = t // tm
    n_groups = -(-n_tiles // DISPATCH_TILES)
    out_blk = pl.BlockSpec((tm, ROUTER_LANES), lambda p, i: (i * p, 0))
    sub = 8
    vec = pltpu.VMEM((sub, ROUTER_LANES), F32)
    return pl.pallas_call(
        _route_kernel,
        grid=(2, n_tiles),
        in_specs=[pl.BlockSpec((tm, ROUTER_LANES), lambda p, i: (i, 0))],
        out_specs=[out_blk, out_blk, pl.BlockSpec((sub, ROUTER_LANES), lambda p, i: (0, 0)),
                   pl.BlockSpec((None, sub, ROUTER_LANES), lambda p, i: (i * p, 0, 0)),
                   pl.BlockSpec((None, sub, ROUTER_LANES), lambda p, i: (i // DISPATCH_TILES * p, 0, 0))],
        out_shape=[jax.ShapeDtypeStruct((t, ROUTER_LANES), jnp.int32),
                   jax.ShapeDtypeStruct((t, ROUTER_LANES), F32),
                   jax.ShapeDtypeStruct((sub, ROUTER_LANES), F32),
                   jax.ShapeDtypeStruct((n_tiles, sub, ROUTER_LANES), jnp.int32),
                   jax.ShapeDtypeStruct((n_groups, sub, ROUTER_LANES), jnp.int32)],
        scratch_shapes=[vec, vec, vec, vec, pltpu.VMEM((n_groups, sub, ROUTER_LANES), F32)],
        compiler_params=_params(("arbitrary", "arbitrary")),
        name="route",
    )(logits)


def _dispatch_kernel(dst_ref, cnt_ref, src_ref, tail_ref, tail_cnt_ref, valid_blocks_ref, hp_ref, hs_ref, pos_ref, xs_hbm,
                     buf, zeros, sem, tail_sem, *, n_tokens, n_prompt_groups):
    g = pl.program_id(0)
    n = pl.num_programs(0)
    tg = hp_ref.shape[0]
    n_buf = buf.shape[1]
    slot = g % 2

    def tail_copies(wait):
        def go(cp):
            if wait:
                cp.wait()
            else:
                cp.start()
            return 0

        def per_expert(e, _):
            def per_piece(c, _):
                rows = pl.ds(pl.multiple_of((tail_ref[e] + c) * RUN_ROWS, RUN_ROWS), RUN_ROWS)
                return go(pltpu.make_async_copy(zeros.at[pl.ds(0, RUN_ROWS), :], xs_hbm.at[rows, :], tail_sem))

            lax.fori_loop(0, tail_cnt_ref[e], per_piece, 0)
            return 0

        def per_block(b, _):
            rows = pl.ds(pl.multiple_of(b * MOE_BLOCK, MOE_BLOCK), MOE_BLOCK)
            return go(pltpu.make_async_copy(zeros, xs_hbm.at[rows, :], tail_sem))

        lax.fori_loop(0, N_EXPERTS, per_expert, 0)
        lax.fori_loop(valid_blocks_ref[0], xs_hbm.shape[0] // MOE_BLOCK, per_block, 0)

    @pl.when(g == 0)
    def _():
        zeros[...] = jnp.zeros_like(zeros)
        tail_copies(wait=False)

    def run_copies(grp, slot, wait):
        def per_expert(e, _):
            idx = grp * N_EXPERTS + e
            dst, src = dst_ref[idx], src_ref[idx]

            def per_piece(c, _):
                cp = pltpu.make_async_copy(
                    buf.at[slot, pl.ds(pl.multiple_of((src + c) * RUN_ROWS, RUN_ROWS), RUN_ROWS), :],
                    xs_hbm.at[pl.ds(pl.multiple_of((dst + c) * RUN_ROWS, RUN_ROWS), RUN_ROWS), :],
                    sem.at[slot])
                if wait:
                    cp.wait()
                else:
                    cp.start()
                return 0

            lax.fori_loop(0, cnt_ref[idx], per_piece, 0)
            return 0

        lax.fori_loop(0, N_EXPERTS, per_expert, 0)

    @pl.when(g >= 2)
    def _():
        run_copies(g - 2, slot, wait=True)

    pos_t = pos_ref[...].T
    tok = g * tg + lax.broadcasted_iota(jnp.int32, (1, tg), 1)
    rows_of = [jnp.where(tok < n_tokens, pos_t[2 * TOP_K + k:2 * TOP_K + k + 1, :], -1) for k in range(TOP_K)]
    h = jnp.where(g < n_prompt_groups, hp_ref[...], hs_ref[...])
    for c in range(n_buf // DISPATCH_CHUNK):
        r = c * DISPATCH_CHUNK + lax.broadcasted_iota(jnp.int32, (DISPATCH_CHUNK, tg), 0)
        sel = jnp.zeros((DISPATCH_CHUNK, tg), F32)
        for k in range(TOP_K):
            sel = jnp.where(r == rows_of[k], 1.0, sel)
        buf[slot, c * DISPATCH_CHUNK:(c + 1) * DISPATCH_CHUNK, :] = _dot(sel.astype(BF16), h).astype(BF16)

    run_copies(g, slot, wait=False)

    @pl.when(g == 0)
    def _():
        tail_copies(wait=True)

    @pl.when(jnp.logical_and(g == n - 1, g >= 1))
    def _():
        run_copies(g - 1, 1 - slot, wait=True)

    @pl.when(g == n - 1)
    def _():
        run_copies(g, slot, wait=True)


def _dispatch(h_prompt, h_sample, pos, group_tables, tail_tables, n_slots, tg):
    tp, ts = h_prompt.shape[0], h_sample.shape[0]
    assert tp % tg == 0 and ts <= tg
    n_prompt_groups = tp // tg
    h_sample = jnp.pad(h_sample, ((0, tg - ts), (0, 0)))
    n_buf = tg * TOP_K + N_EXPERTS * RUN_ROWS
    grid_spec = pltpu.PrefetchScalarGridSpec(
        num_scalar_prefetch=6,
        grid=(n_prompt_groups + 1,),
        in_specs=[pl.BlockSpec((tg, D_MODEL), lambda g, *_: (jnp.minimum(g, n_prompt_groups - 1), 0)),
                  pl.BlockSpec((tg, D_MODEL), lambda g, *_: (0, 0)),
                  pl.BlockSpec((tg, ROUTER_LANES), lambda g, *_: (g, 0))],
        out_specs=pl.BlockSpec(memory_space=pl.ANY),
        scratch_shapes=[pltpu.VMEM((2, n_buf, D_MODEL), BF16), pltpu.VMEM((MOE_BLOCK, D_MODEL), BF16),
                        pltpu.SemaphoreType.DMA((2,)), pltpu.SemaphoreType.DMA(())],
    )
    return pl.pallas_call(
        functools.partial(_dispatch_kernel, n_tokens=tp + ts, n_prompt_groups=n_prompt_groups),
        grid_spec=grid_spec,
        out_shape=jax.ShapeDtypeStruct((n_slots, D_MODEL), BF16),
        compiler_params=_params(("arbitrary",)),
        name="dispatch",
    )(*group_tables, *tail_tables, h_prompt, h_sample, pos)


def _block_tables(counts, n_blocks):
    padded = (counts + MOE_BLOCK - 1) // MOE_BLOCK * MOE_BLOCK
    pend = jnp.cumsum(padded)
    bstart = jnp.arange(n_blocks, dtype=jnp.int32) * MOE_BLOCK
    block_exp = jnp.minimum(jnp.sum((pend[None, :] <= bstart[:, None]).astype(jnp.int32), axis=1), N_EXPERTS - 1)
    block_valid = (bstart < pend[-1]).astype(jnp.int32)
    used = (counts + RUN_ROWS - 1) // RUN_ROWS
    tail_tables = ((pend - padded) // RUN_ROWS + used, padded // RUN_ROWS - used, pend[-1:] // MOE_BLOCK)
    return block_exp, block_valid, tail_tables


def _layer(l, xp, xs, kv_prompt, kv_sample, cache_k, cache_v, w):
    bsz, seq, _ = xp.shape
    dbsz, dseq, _ = xs.shape
    tp, ts = bsz * seq, dbsz * dseq

    def dense(x2, kv, n_stream, n_seq, chunk, attn_fn):
        q, k_all, v_all, kb, vb, u, zn, znb = _inproj(x2, w['ln_mix_g'][l], w['w_in'][l], w['sgu_ln_g'][l],
                                                      w['sgu_ln_b'][l], w['gavg'], kv, l, w['depth'])
        r3 = lambda a: a.reshape(n_stream, n_seq, A_WIDTH)
        oa = attn_fn(r3(q), r3(kb), r3(vb)).reshape(n_stream * n_seq, A_WIDTH)
        wsp = w['w_spatial'][l][:, :chunk, :chunk]
        bsp = w['b_spatial'][l][:chunk]
        x1, h2, lg = _mixout(x2, oa, u, znb, wsp, bsp, w['out_norm_a_g'][l], w['out_norm_b_g'][l],
                             w['w_out'][l], w['ln_ffn_g'][l], w['wr_hi'][l], w['wr_lo'][l], w['b_router'][l])
        return x1, h2, lg, (k_all, v_all), zn

    x1p, h2p, lgp, kv_prompt, _ = dense(xp.reshape(tp, D_MODEL), kv_prompt, bsz, seq, MLP_CHUNK, _attn_prompt)
    x1s, h2s, lgs, kv_sample, zs = dense(xs.reshape(ts, D_MODEL), kv_sample, dbsz, dseq, dseq,
                                         lambda q, k, v: _attn_sample(q, k, v, cache_k, cache_v, l))

    t = tp + ts
    slots, gates, counts, runs, group_runs = _route(jnp.concatenate([lgp, lgs], axis=0))
    route_tile = t // runs.shape[0]
    n_groups = group_runs.shape[0]
    tg = route_tile * DISPATCH_TILES
    tables = lambda a: tuple(a[:, j, :N_EXPERTS].reshape(-1) for j in range(3))
    run_tables, group_tables = tables(runs), tables(group_runs)
    max_rows = t * TOP_K + n_groups * N_EXPERTS * (RUN_ROWS - 1)
    n_blocks = -(-max_rows // MOE_BLOCK) + N_EXPERTS + 1
    block_exp, block_valid, tail_tables = _block_tables(counts[0, :N_EXPERTS].astype(jnp.int32), n_blocks)
    x_sorted = _dispatch(h2p, h2s, slots, group_tables, tail_tables, n_blocks * MOE_BLOCK, tg)
    out = _moe(block_exp, block_valid, x_sorted, w['w_gate_up'], w['b_gate_up'], w['w_down'], w['b_down'], l)
    final = l == w['depth'] - 1
    yp = _collect(x1p, out, run_tables, slots, gates, w['final_norm_g'], final, 0, route_tile)
    ys = _collect(x1s, out, run_tables, slots, gates, w['final_norm_g'], final, tp, route_tile)
    return (yp.reshape(bsz, seq, D_MODEL), ys.reshape(dbsz, dseq, D_MODEL), kv_prompt, kv_sample,
            zs.reshape(dbsz, dseq, N_B_GROUPS, B_GROUP_DIM))


def _prepare(ln_mix_g, w_in, sgu_ln_g, sgu_ln_b, w_spatial, b_spatial, out_norm_a_g, out_norm_b_g, w_out,
             ln_ffn_g, w_router, b_router, w_gate_up, b_gate_up, w_down, b_down, final_norm_g):
    depth = w_in.shape[0]
    row = lambda a: a.reshape(depth, 1, -1)
    blk = jnp.arange(MLP_CHUNK, dtype=jnp.int32) // CHUNK
    mask = blk[None, :] <= blk[:, None]
    grp = jnp.arange(B_WIDTH, dtype=jnp.int32) // B_GROUP_DIM
    gavg = jnp.where(grp[:, None] == grp[None, :], 1.0 / B_GROUP_DIM, 0.0).astype(BF16)
    wr = jnp.pad(w_router, ((0, 0), (0, 0), (0, ROUTER_LANES - N_EXPERTS)))
    wr_hi = wr.astype(BF16)
    wr_lo = (wr - wr_hi.astype(F32)).astype(BF16)
    return dict(
        depth=depth,
        ln_mix_g=row(ln_mix_g), w_in=w_in.astype(BF16), sgu_ln_g=row(sgu_ln_g), sgu_ln_b=row(sgu_ln_b),
        gavg=gavg,
        w_spatial=jnp.where(mask[None, None], w_spatial, 0.0).astype(BF16),
        b_spatial=jnp.repeat(jnp.swapaxes(b_spatial, 1, 2), B_GROUP_DIM, axis=2),
        out_norm_a_g=row(out_norm_a_g), out_norm_b_g=row(out_norm_b_g), w_out=w_out.astype(BF16),
        ln_ffn_g=row(ln_ffn_g), wr_hi=wr_hi, wr_lo=wr_lo,
        b_router=row(jnp.pad(b_router, ((0, 0), (0, ROUTER_LANES - N_EXPERTS)))),
        w_gate_up=w_gate_up, b_gate_up=b_gate_up[:, :, None, :],
        w_down=w_down, b_down=b_down[:, :, None, :],
        final_norm_g=final_norm_g.reshape(1, -1),
    )


def kernel(x_prompt, x_sample, cache_k, cache_v, ln_mix_g, w_in, sgu_ln_g, sgu_ln_b, w_spatial, b_spatial,
           out_norm_a_g, out_norm_b_g, w_out, ln_ffn_g, w_router, b_router, w_gate_up, b_gate_up, w_down,
           b_down, final_norm_g):
    w = _prepare(ln_mix_g, w_in, sgu_ln_g, sgu_ln_b, w_spatial, b_spatial, out_norm_a_g, out_norm_b_g, w_out,
                 ln_ffn_g, w_router, b_router, w_gate_up, b_gate_up, w_down, b_down, final_norm_g)
    depth, dbsz, past = cache_k.shape[:3]
    ck = cache_k.reshape(depth, dbsz, past, A_WIDTH)
    cv = cache_v.reshape(depth, dbsz, past, A_WIDTH)
    xp, xs = x_prompt, x_sample
    bsz, seq, _ = xp.shape
    dseq = xs.shape[1]
    kv_prompt = kv_sample = None
    zss = []
    for l in range(depth):
        xp, xs, kv_prompt, kv_sample, zs = _layer(l, xp, xs, kv_prompt, kv_sample, ck, cv, w)
        zss.append(zs)
    heads = lambda a, n_stream, n_seq: a.reshape(depth, n_stream, n_seq, N_A_HEADS, HEAD_DIM)
    return (xp, xs, heads(kv_prompt[0], bsz, seq), heads(kv_prompt[1], bsz, seq),
            heads(kv_sample[0], dbsz, dseq), heads(kv_sample[1], dbsz, dseq), jnp.stack(zss))
```

```python
import functools
import math

import jax
import jax.numpy as jnp
from jax import lax
from jax.experimental import pallas as pl
from jax.experimental.pallas import tpu as pltpu

F32 = jnp.float32
BF16 = jnp.bfloat16

D_MODEL = 1024
HEAD_DIM = 64
N_A_HEADS = 8
A_WIDTH = N_A_HEADS * HEAD_DIM
N_B_GROUPS = 8
B_GROUP_DIM = 64
B_WIDTH = N_B_GROUPS * B_GROUP_DIM
CHUNK = 64
MLP_CHUNK = 128
SB_SCALE = HEAD_DIM ** -0.5
N_EXPERTS = 32
TOP_K = 4
D_FF = D_MODEL
SWIGLU_LIMIT = 7.0
SWIGLU_ALPHA = 1.702
NORM_EPS = 1e-5

LANES = 128
HEADS_PER_TILE = LANES // HEAD_DIM
N_HEAD_TILES = A_WIDTH // LANES
ROUTER_LANES = 128
VMEM_LIMIT = 56 * 1024 * 1024
TOKEN_TILE = 256
DENSE_TILE = 512
ATTN_Q_TILE = 256
ATTN_K_TILE = 256
MOE_BLOCK = 512
WEIGHT_CAST_ROWS = 128
FF_CHUNK = 256
RUN_ROWS = 16
DISPATCH_TILES = 4
DISPATCH_CHUNK = 512
ZERO_WEIGHT_LOG = -110.0


def _params(sem):
    return pltpu.CompilerParams(dimension_semantics=sem, vmem_limit_bytes=VMEM_LIMIT)


def _split_bf16(t):
    hi = t.astype(BF16)
    lo = (t - hi.astype(F32)).astype(BF16)
    return hi, lo


def _dot(a, b):
    return jnp.dot(a, b, preferred_element_type=F32)


def _dot_nt(a, b):
    return lax.dot_general(a, b, (((1,), (1,)), ((), ())), preferred_element_type=F32)


def _rms(x, g):
    return x * lax.rsqrt(jnp.mean(x * x, axis=-1, keepdims=True) + NORM_EPS) * g


def _inproj_kernel(*refs, first):
    x_ref, g_ref, w_ref, lng_ref, lnb_ref, gavg_ref = refs[:6]
    q_ref, k_ref, v_ref, kb_ref, vb_ref, u_ref, zn_ref, znb_ref = refs[-8:]
    hn = _rms(x_ref[...], g_ref[...]).astype(BF16)

    def proj(sec):
        return _dot(hn, w_ref[:, sec * A_WIDTH:(sec + 1) * A_WIDTH])

    def put(all_ref, val):
        if first:
            all_ref[0] = val
            if all_ref.shape[0] > 1:
                all_ref[1:] = jnp.zeros((all_ref.shape[0] - 1,) + val.shape, F32)
        else:
            all_ref[...] = val

    q_ref[...] = (proj(0) * SB_SCALE).astype(BF16)
    k = proj(1)
    put(k_ref, k)
    kb_ref[...] = k.astype(BF16)
    v = proj(2)
    put(v_ref, v)
    vb_ref[...] = v.astype(BF16)
    u_ref[...] = jax.nn.gelu(proj(3)).astype(BF16)
    z = jax.nn.gelu(proj(4))
    gavg = gavg_ref[...]

    def group_mean(t):
        hi, lo = _split_bf16(t)
        return _dot(hi, gavg) + _dot(lo, gavg)

    zc = z - group_mean(z)
    var = group_mean(zc * zc)
    zn = zc * lax.rsqrt(var + NORM_EPS) * lng_ref[...] + lnb_ref[...]
    zn_ref[...] = zn
    znb_ref[...] = zn.astype(BF16)


def _inproj(x, g, w_bf, lng, lnb, gavg, kv_all, layer, depth):
    t = x.shape[0]
    tm = min(DENSE_TILE, t)
    row = lambda w: pl.BlockSpec((tm, w), lambda i: (i, 0))
    full = lambda a: pl.BlockSpec(a.shape, lambda i: (0,) * a.ndim)
    sds = lambda w, dt: jax.ShapeDtypeStruct((t, w), dt)
    first = kv_all is None
    if first:
        extra_specs, extra_args, aliases = [], (), {}
        layer_rows = pl.BlockSpec((depth, tm, A_WIDTH), lambda i: (0, i, 0))
    else:
        extra_specs, extra_args, aliases = [pl.BlockSpec(memory_space=pl.ANY)] * 2, tuple(kv_all), {6: 1, 7: 2}
        layer_rows = pl.BlockSpec((None, tm, A_WIDTH), lambda i: (layer, i, 0))
    all_sds = jax.ShapeDtypeStruct((depth, t, A_WIDTH), F32)
    return pl.pallas_call(
        functools.partial(_inproj_kernel, first=first),
        grid=(t // tm,),
        in_specs=[row(D_MODEL), full(g), full(w_bf), full(lng), full(lnb), full(gavg)] + extra_specs,
        out_specs=[row(A_WIDTH), layer_rows, layer_rows] + [row(A_WIDTH)] * 5,
        out_shape=[sds(A_WIDTH, BF16), all_sds, all_sds, sds(A_WIDTH, BF16),
                   sds(A_WIDTH, BF16), sds(B_WIDTH, BF16), sds(B_WIDTH, F32), sds(B_WIDTH, BF16)],
        input_output_aliases=aliases,
        compiler_params=_params(("parallel",)),
        name="inproj",
    )(x, g, w_bf, lng, lnb, gavg, *extra_args)


def _log_terms(z):
    soft = jnp.log(1.0 + jnp.exp(-jnp.abs(z)))
    return jnp.minimum(z, 0.0) - soft, jnp.minimum(-z, 0.0) - soft


def _tri(n):
    j = lax.broadcasted_iota(jnp.int32, (n, n), 0)
    s = lax.broadcasted_iota(jnp.int32, (n, n), 1)
    return jnp.where(j > s, 1.0, 0.0).astype(BF16)


def _later_sum(log_keep, tri):
    hi, lo = _split_bf16(log_keep)
    return _dot(hi, tri) + _dot(lo, tri)


def _attn_kernel(q_ref, kd_ref, vd_ref, kp_ref, vp_ref, o_ref, *, n_past_fn, tk):
    tq = q_ref.shape[0]
    q = q_ref[...].astype(F32)
    lane = lax.broadcasted_iota(jnp.int32, (tq, LANES), 1)
    qs = [jnp.where((lane // HEAD_DIM) == h, q, 0.0).astype(BF16) for h in range(HEADS_PER_TILE)]

    kd = kd_ref[...].astype(BF16)
    vd = vd_ref[...].astype(BF16)
    t_pos = lax.broadcasted_iota(jnp.int32, (tq, tq), 0)
    s_pos = lax.broadcasted_iota(jnp.int32, (tq, tq), 1)
    visible = s_pos < t_pos
    tri_d = _tri(tq)
    accs, carries = [], []
    for qh in qs:
        log_beta, log_keep = _log_terms(_dot_nt(qh, kd))
        log_keep = jnp.where(visible, log_keep, 0.0)
        a = jnp.where(visible, jnp.exp(log_beta + _later_sum(log_keep, tri_d)), 0.0)
        accs.append(_dot(a.astype(BF16), vd))
        carries.append(jnp.sum(log_keep, axis=-1, keepdims=True))

    n_past = n_past_fn()
    tri_p = _tri(tk)

    def carry_max(carries):
        return functools.reduce(jnp.maximum, [jnp.max(c) for c in carries])

    def cond(state):
        j, _, _, cmax = state
        return jnp.logical_and(j < n_past, cmax > ZERO_WEIGHT_LOG)

    def past_block(j, accs, carries, live=None):
        start = pl.multiple_of(jnp.maximum(n_past - 1 - j, 0) * tk, tk)
        kp = kp_ref[pl.ds(start, tk), :].astype(BF16)
        vp = vp_ref[pl.ds(start, tk), :].astype(BF16)
        new_accs, new_carries = [], []
        for qh, acc, carry in zip(qs, accs, carries):
            log_beta, log_keep = _log_terms(_dot_nt(qh, kp))
            a = jnp.exp(log_beta + _later_sum(log_keep, tri_p) + carry)
            av = _dot(a.astype(BF16), vp)
            new_accs.append(acc + (av if live is None else jnp.where(live, av, 0.0)))
            new_carries.append(carry + jnp.sum(log_keep, axis=-1, keepdims=True))
        return tuple(new_accs), tuple(new_carries)

    def body(state):
        j, accs, carries, _ = state
        accs, carries = past_block(j, accs, carries)
        return j + 1, accs, carries, carry_max(carries)

    accs, carries = past_block(0, accs, carries, live=n_past > 0)
    _, accs, _, _ = lax.while_loop(cond, body, (jnp.int32(1), accs, carries, carry_max(carries)))
    out = accs[0]
    for h in range(1, HEADS_PER_TILE):
        out = jnp.where((lane // HEAD_DIM) == h, accs[h], out)
    o_ref[...] = out


def _attn_prompt(q_bf, k_bf, v_bf):
    b, s, _ = q_bf.shape
    tq = min(ATTN_Q_TILE, s)
    tk = tq
    blk = pl.BlockSpec((None, tq, LANES), lambda bi, hp, i: (bi, i, hp))
    seq = pl.BlockSpec((None, s, LANES), lambda bi, hp, i: (bi, 0, hp))
    kern = functools.partial(_attn_kernel, n_past_fn=lambda: pl.program_id(2) * (tq // tk), tk=tk)
    return pl.pallas_call(
        kern,
        grid=(b, N_HEAD_TILES, s // tq),
        in_specs=[blk, blk, blk, seq, seq],
        out_specs=blk,
        out_shape=jax.ShapeDtypeStruct((b, s, A_WIDTH), F32),
        compiler_params=_params(("parallel", "parallel", "arbitrary")),
        name="attn_prompt",
    )(q_bf, k_bf, v_bf, k_bf, v_bf)


def _attn_sample(q_bf, k_bf, v_bf, cache_k, cache_v, layer):
    b, n, _ = q_bf.shape
    p = cache_k.shape[2]
    tk = min(ATTN_K_TILE, p)
    blk = pl.BlockSpec((None, n, LANES), lambda bi, hp: (bi, 0, hp))
    past = pl.BlockSpec((None, None, p, LANES), lambda bi, hp: (layer, bi, 0, hp))
    kern = functools.partial(_attn_kernel, n_past_fn=lambda: p // tk, tk=tk)
    return pl.pallas_call(
        kern,
        grid=(b, N_HEAD_TILES),
        in_specs=[blk, blk, blk, past, past],
        out_specs=blk,
        out_shape=jax.ShapeDtypeStruct((b, n, A_WIDTH), F32),
        compiler_params=_params(("parallel", "parallel")),
        name="attn_sample",
    )(q_bf, k_bf, v_bf, cache_k, cache_v)


def _mixout_kernel(x_ref, oa_ref, u_ref, znb_ref, wsp_ref, bsp_ref, ga_ref, gb_ref, wout_ref,
                   lnf_ref, wrh_ref, wrl_ref, br_ref, x1_ref, h2_ref, lg_ref, mix_ref):
    tm = x_ref.shape[0]
    chunk = wsp_ref.shape[1]
    group = lax.broadcasted_iota(jnp.int32, (chunk, B_WIDTH), 1) // B_GROUP_DIM
    for c in range(tm // chunk):
        zc = znb_ref[c * chunk:(c + 1) * chunk, :]
        m = jnp.zeros((chunk, B_WIDTH), F32)
        for g in range(N_B_GROUPS):
            m = jnp.where(group == g, _dot(wsp_ref[g], zc), m)
        mix_ref[c * chunk:(c + 1) * chunk, :] = m + bsp_ref[...]
    ob = u_ref[...].astype(F32) * mix_ref[...]
    ya = _rms(oa_ref[...], ga_ref[...]).astype(BF16)
    yb = _rms(ob, gb_ref[...]).astype(BF16)
    x1 = x_ref[...] + _dot(ya, wout_ref[:A_WIDTH, :]) + _dot(yb, wout_ref[A_WIDTH:, :])
    x1_ref[...] = x1
    h2 = _rms(x1, lnf_ref[...])
    hi, lo = _split_bf16(h2)
    h2_ref[...] = hi
    wrh = wrh_ref[...]
    lg_ref[...] = _dot(hi, wrh) + _dot(lo, wrh) + _dot(hi, wrl_ref[...]) + br_ref[...]


def _mixout(x, oa, u, znb, wsp, bsp, ga, gb, wout_bf, lnf, wrh, wrl, br):
    t = x.shape[0]
    tm = min(DENSE_TILE, t)
    row = lambda w: pl.BlockSpec((tm, w), lambda i: (i, 0))
    full = lambda a: pl.BlockSpec(a.shape, lambda i: (0,) * a.ndim)
    return pl.pallas_call(
        _mixout_kernel,
        grid=(t // tm,),
        in_specs=[row(D_MODEL), row(A_WIDTH), row(B_WIDTH), row(B_WIDTH), full(wsp), full(bsp),
                  full(ga), full(gb), full(wout_bf), full(lnf), full(wrh), full(wrl), full(br)],
        out_specs=[row(D_MODEL), row(D_MODEL), row(ROUTER_LANES)],
        out_shape=[jax.ShapeDtypeStruct((t, D_MODEL), F32), jax.ShapeDtypeStruct((t, D_MODEL), BF16),
                   jax.ShapeDtypeStruct((t, ROUTER_LANES), F32)],
        scratch_shapes=[pltpu.VMEM((tm, B_WIDTH), F32)],
        compiler_params=_params(("parallel",)),
        name="mixout",
    )(x, oa, u, znb, wsp, bsp, ga, gb, wout_bf, lnf, wrh, wrl, br)


def _moe_kernel(bexp_ref, bval_ref, xs_ref, wgu_ref, bgu_ref, wd_ref, bd_ref, o_ref, wgu_bf, wd_bf):
    i = pl.program_id(0)
    valid = bval_ref[i] > 0
    new_expert = jnp.logical_or(i == 0, bexp_ref[i] != bexp_ref[jnp.maximum(i - 1, 0)])

    @pl.when(jnp.logical_and(valid, new_expert))
    def _():
        def cast_rows(r, _):
            rows = pl.ds(pl.multiple_of(r * WEIGHT_CAST_ROWS, WEIGHT_CAST_ROWS), WEIGHT_CAST_ROWS)
            for c in range(D_FF // FF_CHUNK):
                lo, hi = c * FF_CHUNK, (c + 1) * FF_CHUNK
                wgu_bf[rows, 2 * lo:2 * lo + FF_CHUNK] = wgu_ref[rows, lo:hi].astype(BF16)
                wgu_bf[rows, 2 * lo + FF_CHUNK:2 * hi] = wgu_ref[rows, D_FF + lo:D_FF + hi].astype(BF16)
            wd_bf[rows, :] = wd_ref[rows, :].astype(BF16)
            return 0
        lax.fori_loop(0, D_MODEL // WEIGHT_CAST_ROWS, cast_rows, 0)

    @pl.when(valid)
    def _():
        xs = xs_ref[...]
        acc = None
        for c in range(D_FF // FF_CHUNK):
            lo, hi = c * FF_CHUNK, (c + 1) * FF_CHUNK
            gate_lin = _dot(xs, wgu_bf[:, 2 * lo:2 * hi])
            gate = gate_lin[:, :FF_CHUNK] + bgu_ref[:, lo:hi]
            lin = gate_lin[:, FF_CHUNK:] + bgu_ref[:, D_FF + lo:D_FF + hi]
            gate = jnp.minimum(gate, SWIGLU_LIMIT)
            lin = jnp.clip(lin, -SWIGLU_LIMIT, SWIGLU_LIMIT)
            act = gate * jax.nn.sigmoid(SWIGLU_ALPHA * gate) * (lin + 1.0)
            part = _dot(act.astype(BF16), wd_bf[lo:hi, :])
            acc = part if acc is None else acc + part
        o_ref[...] = (acc + bd_ref[...]).astype(o_ref.dtype)

    @pl.when(bval_ref[i] == 0)
    def _():
        o_ref[...] = jnp.zeros_like(o_ref)


def _moe(block_exp, block_valid, xs, wgu, bgu, wd, bd, layer):
    n_slots = xs.shape[0]
    bm = MOE_BLOCK
    grid_spec = pltpu.PrefetchScalarGridSpec(
        num_scalar_prefetch=2,
        grid=(n_slots // bm,),
        in_specs=[
            pl.BlockSpec((bm, D_MODEL), lambda i, be, bv: (i, 0)),
            pl.BlockSpec((None, None, D_MODEL, 2 * D_FF), lambda i, be, bv: (layer, be[i], 0, 0)),
            pl.BlockSpec((None, None, 1, 2 * D_FF), lambda i, be, bv: (layer, be[i], 0, 0)),
            pl.BlockSpec((None, None, D_FF, D_MODEL), lambda i, be, bv: (layer, be[i], 0, 0)),
            pl.BlockSpec((None, None, 1, D_MODEL), lambda i, be, bv: (layer, be[i], 0, 0)),
        ],
        out_specs=pl.BlockSpec((bm, D_MODEL), lambda i, be, bv: (i, 0)),
        scratch_shapes=[pltpu.VMEM((D_MODEL, 2 * D_FF), BF16), pltpu.VMEM((D_FF, D_MODEL), BF16)],
    )
    return pl.pallas_call(
        _moe_kernel,
        grid_spec=grid_spec,
        out_shape=jax.ShapeDtypeStruct((n_slots, D_MODEL), BF16),
        compiler_params=_params(("arbitrary",)),
        name="moe",
    )(block_exp, block_valid, xs, wgu, bgu, wd, bd)


def _collect_kernel(src_ref, cnt_ref, dst_ref, x_ref, pos_ref, gates_ref, g_ref, rows_hbm, y_ref, buf, sem,
                    *, final, tile_off):
    i = pl.program_id(0)
    n = pl.num_programs(0)

    def run_copies(tile, slot, wait):
        def per_expert(e, _):
            idx = (tile + tile_off) * N_EXPERTS + e
            src, dst = src_ref[idx], dst_ref[idx]

            def per_piece(c, _):
                cp = pltpu.make_async_copy(
                    rows_hbm.at[pl.ds(pl.multiple_of((src + c) * RUN_ROWS, RUN_ROWS), RUN_ROWS), :],
                    buf.at[slot, pl.ds(pl.multiple_of((dst + c) * RUN_ROWS, RUN_ROWS), RUN_ROWS), :],
                    sem.at[slot])
                if wait:
                    cp.wait()
                else:
                    cp.start()
                return 0

            lax.fori_loop(0, cnt_ref[idx], per_piece, 0)
            return 0

        lax.fori_loop(0, N_EXPERTS, per_expert, 0)

    @pl.when(i == 0)
    def _():
        buf[...] = jnp.zeros_like(buf)
        run_copies(0, 0, wait=False)

    @pl.when(i + 1 < n)
    def _():
        run_copies(i + 1, (i + 1) % 2, wait=False)

    slot = i % 2
    run_copies(i, slot, wait=True)

    tm = x_ref.shape[0]
    n_buf = buf.shape[1]
    pos = pos_ref[...]
    gates = gates_ref[...]
    r = lax.broadcasted_iota(jnp.int32, (tm, n_buf), 1)
    sel = jnp.zeros((tm, n_buf), F32)
    for k in range(TOP_K):
        sel = jnp.where(r == pos[:, TOP_K + k:TOP_K + k + 1], gates[:, k:k + 1], sel)
    hi, lo = _split_bf16(sel)
    rows = buf[slot]
    y = x_ref[...] + _dot(hi, rows) + _dot(lo, rows)
    y_ref[...] = _rms(y, g_ref[...]) if final else y


def _collect(x1, rows, runs, pos, gates, g, final, row_offset, tm):
    t = x1.shape[0]
    off = row_offset // tm
    n_buf = tm * TOP_K + 2 * N_EXPERTS * RUN_ROWS
    grid_spec = pltpu.PrefetchScalarGridSpec(
        num_scalar_prefetch=3,
        grid=(t // tm,),
        in_specs=[pl.BlockSpec((tm, D_MODEL), lambda i, *_: (i, 0)),
                  pl.BlockSpec((tm, ROUTER_LANES), lambda i, *_: (i + off, 0)),
                  pl.BlockSpec((tm, ROUTER_LANES), lambda i, *_: (i + off, 0)),
                  pl.BlockSpec(g.shape, lambda i, *_: (0, 0)),
                  pl.BlockSpec(memory_space=pl.ANY)],
        out_specs=pl.BlockSpec((tm, D_MODEL), lambda i, *_: (i, 0)),
        scratch_shapes=[pltpu.VMEM((2, n_buf, D_MODEL), BF16), pltpu.SemaphoreType.DMA((2,))],
    )
    return pl.pallas_call(
        functools.partial(_collect_kernel, final=final, tile_off=off),
        grid_spec=grid_spec,
        out_shape=jax.ShapeDtypeStruct((t, D_MODEL), F32),
        compiler_params=_params(("arbitrary",)),
        name="collect",
    )(*runs, x1, pos, gates, g, rows)


def _lane_prefix_sum(x):
    lane = lax.broadcasted_iota(jnp.int32, x.shape, 1)
    shift = 1
    while shift < ROUTER_LANES:
        x = x + jnp.where(lane >= shift, pltpu.roll(x, shift, axis=1), 0.0)
        shift *= 2
    return x


def _route_kernel(lg_ref, slots_ref, gates_ref, counts_ref, runs_ref, group_runs_ref,
                  base_ref, pstart_ref, group_start_ref, group_buf_ref, hist_ref, picks_ref):
    phase = pl.program_id(0)
    i = pl.program_id(1)
    group = i // DISPATCH_TILES
    first_in_group = i % DISPATCH_TILES == 0
    pieces_of = lambda rows: jnp.ceil(rows * (1.0 / RUN_ROWS))
    tm = lg_ref.shape[0]
    lane = lax.broadcasted_iota(jnp.int32, (tm, ROUTER_LANES), 1)
    lane_f = lane.astype(F32)
    def counts_of(sel):
        return jnp.sum(sel, axis=0, keepdims=True)

    @pl.when(jnp.logical_and(phase == 0, i == 0))
    def _():
        base_ref[...] = jnp.zeros_like(base_ref)

    @pl.when(jnp.logical_and(phase == 0, first_in_group))
    def _():
        base_ref[...] = pieces_of(base_ref[...]) * RUN_ROWS
        hist_ref[group] = jnp.zeros_like(base_ref)

    @pl.when(phase == 0)
    def _():
        x = jnp.where(lane < N_EXPERTS, lg_ref[...], -jnp.inf)
        sel = jnp.zeros((tm, ROUTER_LANES), F32)
        picks = jnp.zeros((tm, ROUTER_LANES), F32)
        for k in range(TOP_K):
            m = jnp.max(x, axis=-1, keepdims=True)
            idx = jnp.min(jnp.where(x == m, lane_f, float(ROUTER_LANES)), axis=-1, keepdims=True)
            hit = lane_f == idx
            x = jnp.where(hit, -jnp.inf, x)
            sel = jnp.where(hit, 1.0, sel)
            picks = jnp.where(lane == k, m, jnp.where(lane == TOP_K + k, idx, picks))
        picks_ref[i] = picks
        tile_counts = counts_of(sel)
        base_ref[...] = base_ref[...] + tile_counts
        hist_ref[group] = hist_ref[group] + tile_counts

    @pl.when(jnp.logical_and(phase == 1, i == 0))
    def _():
        counts = base_ref[...]
        counts_ref[...] = counts
        padded = jnp.ceil(counts * (1.0 / MOE_BLOCK)) * MOE_BLOCK
        pstart_ref[...] = _lane_prefix_sum(padded) - padded
        base_ref[...] = jnp.zeros_like(base_ref)

    @pl.when(jnp.logical_and(phase == 1, first_in_group))
    def _():
        base = pieces_of(base_ref[...]) * RUN_ROWS
        base_ref[...] = base
        group_start = base + pstart_ref[...]
        group_pieces = pieces_of(hist_ref[group])
        group_buf = (_lane_prefix_sum(group_pieces) - group_pieces) * RUN_ROWS
        group_start_ref[...] = group_start
        group_buf_ref[...] = group_buf
        sub = lax.broadcasted_iota(jnp.int32, base.shape, 0)
        table = jnp.where(sub == 0, group_start * (1.0 / RUN_ROWS),
                          jnp.where(sub == 1, group_pieces, group_buf * (1.0 / RUN_ROWS)))
        group_runs_ref[...] = table.astype(jnp.int32)

    @pl.when(phase == 1)
    def _():
        picks = picks_ref[i]
        vals = [picks[:, k:k + 1] for k in range(TOP_K)]
        idxs = [picks[:, TOP_K + k:TOP_K + k + 1] for k in range(TOP_K)]
        sel = jnp.zeros((tm, ROUTER_LANES), F32)
        for k in range(TOP_K):
            sel = jnp.where(lane_f == idxs[k], 1.0, sel)
        tile_counts = counts_of(sel)
        r = lax.broadcasted_iota(jnp.int32, (tm, tm), 0)
        c = lax.broadcasted_iota(jnp.int32, (tm, tm), 1)
        earlier = jnp.where(c < r, 1.0, 0.0).astype(BF16)
        run_start = base_ref[...] + pstart_ref[...]
        pos = _dot(earlier, sel.astype(BF16)) + run_start[0:1, :]
        aligned = jnp.floor(run_start * (1.0 / RUN_ROWS)) * RUN_ROWS
        pieces = jnp.where(tile_counts > 0.0, jnp.ceil((run_start - aligned + tile_counts) * (1.0 / RUN_ROWS)), 0.0)
        buf_base = (_lane_prefix_sum(pieces) - pieces) * RUN_ROWS
        to_buf = (buf_base - aligned)[0:1, :]
        to_group_buf = (group_buf_ref[...] - group_start_ref[...])[0:1, :]
        weights = [jnp.exp(v - vals[0]) for v in vals]
        total = functools.reduce(lambda a, b: a + b, weights)
        slots = jnp.zeros((tm, ROUTER_LANES), jnp.int32)
        gates = jnp.zeros((tm, ROUTER_LANES), F32)
        for k in range(TOP_K):
            mine = lane_f == idxs[k]
            slot_k = jnp.sum(jnp.where(mine, pos, 0.0), axis=-1, keepdims=True)
            buf_k = jnp.sum(jnp.where(mine, pos + to_buf, 0.0), axis=-1, keepdims=True)
            group_buf_k = jnp.sum(jnp.where(mine, pos + to_group_buf, 0.0), axis=-1, keepdims=True)
            slots = jnp.where(lane == k, slot_k.astype(jnp.int32), slots)
            slots = jnp.where(lane == TOP_K + k, buf_k.astype(jnp.int32), slots)
            slots = jnp.where(lane == 2 * TOP_K + k, group_buf_k.astype(jnp.int32), slots)
            gates = jnp.where(lane == k, weights[k] / total, gates)
        slots_ref[...] = slots
        gates_ref[...] = gates
        sub = lax.broadcasted_iota(jnp.int32, run_start.shape, 0)
        runs = jnp.where(sub == 0, aligned * (1.0 / RUN_ROWS),
                         jnp.where(sub == 1, pieces, buf_base * (1.0 / RUN_ROWS)))
        runs_ref[...] = runs.astype(jnp.int32)
        base_ref[...] = base_ref[...] + tile_counts


def _route(logits):
    t = logits.shape[0]
    tm = math.gcd(TOKEN_TILE, t)
    n_tiles = t // tm
    n_groups = -(-n_tiles // DISPATCH_TILES)
    out_blk = pl.BlockSpec((tm, ROUTER_LANES), lambda p, i: (i * p, 0))
    sub = 8
    vec = pltpu.VMEM((sub, ROUTER_LANES), F32)
    return pl.pallas_call(
        _route_kernel,
        grid=(2, n_tiles),
        in_specs=[pl.BlockSpec((tm, ROUTER_LANES), lambda p, i: (i, 0))],
        out_specs=[out_blk, out_blk, pl.BlockSpec((sub, ROUTER_LANES), lambda p, i: (0, 0)),
                   pl.BlockSpec((None, sub, ROUTER_LANES), lambda p, i: (i * p, 0, 0)),
                   pl.BlockSpec((None, sub, ROUTER_LANES), lambda p, i: (i // DISPATCH_TILES * p, 0, 0))],
        out_shape=[jax.ShapeDtypeStruct((t, ROUTER_LANES), jnp.int32),
                   jax.ShapeDtypeStruct((t, ROUTER_LANES), F32),
                   jax.ShapeDtypeStruct((sub, ROUTER_LANES), F32),
                   jax.ShapeDtypeStruct((n_tiles, sub, ROUTER_LANES), jnp.int32),
                   jax.ShapeDtypeStruct((n_groups, sub, ROUTER_LANES), jnp.int32)],
        scratch_shapes=[vec, vec, vec, vec, pltpu.VMEM((n_groups, sub, ROUTER_LANES), F32),
                        pltpu.VMEM((n_tiles, tm, ROUTER_LANES), F32)],
        compiler_params=_params(("arbitrary", "arbitrary")),
        name="route",
    )(logits)


def _dispatch_kernel(dst_ref, cnt_ref, src_ref, tail_ref, tail_cnt_ref, valid_blocks_ref, hp_ref, hs_ref, pos_ref, xs_hbm,
                     buf, zeros, sem, tail_sem, *, n_tokens, n_prompt_groups):
    g = pl.program_id(0)
    n = pl.num_programs(0)
    tg = hp_ref.shape[0]
    n_buf = buf.shape[1]
    slot = g % 2

    def tail_copies(wait):
        def go(cp):
            if wait:
                cp.wait()
            else:
                cp.start()
            return 0

        def per_expert(e, _):
            def per_piece(c, _):
                rows = pl.ds(pl.multiple_of((tail_ref[e] + c) * RUN_ROWS, RUN_ROWS), RUN_ROWS)
                return go(pltpu.make_async_copy(zeros.at[pl.ds(0, RUN_ROWS), :], xs_hbm.at[rows, :], tail_sem))

            lax.fori_loop(0, tail_cnt_ref[e], per_piece, 0)
            return 0

        def per_block(b, _):
            rows = pl.ds(pl.multiple_of(b * MOE_BLOCK, MOE_BLOCK), MOE_BLOCK)
            return go(pltpu.make_async_copy(zeros, xs_hbm.at[rows, :], tail_sem))

        lax.fori_loop(0, N_EXPERTS, per_expert, 0)
        lax.fori_loop(valid_blocks_ref[0], xs_hbm.shape[0] // MOE_BLOCK, per_block, 0)

    @pl.when(g == 0)
    def _():
        zeros[...] = jnp.zeros_like(zeros)
        tail_copies(wait=False)

    def run_copies(grp, slot, wait):
        def per_expert(e, _):
            idx = grp * N_EXPERTS + e
            dst, src = dst_ref[idx], src_ref[idx]

            def per_piece(c, _):
                cp = pltpu.make_async_copy(
                    buf.at[slot, pl.ds(pl.multiple_of((src + c) * RUN_ROWS, RUN_ROWS), RUN_ROWS), :],
                    xs_hbm.at[pl.ds(pl.multiple_of((dst + c) * RUN_ROWS, RUN_ROWS), RUN_ROWS), :],
                    sem.at[slot])
                if wait:
                    cp.wait()
                else:
                    cp.start()
                return 0

            lax.fori_loop(0, cnt_ref[idx], per_piece, 0)
            return 0

        lax.fori_loop(0, N_EXPERTS, per_expert, 0)

    @pl.when(g >= 2)
    def _():
        run_copies(g - 2, slot, wait=True)

    pos_t = pos_ref[...].T
    tok = g * tg + lax.broadcasted_iota(jnp.int32, (1, tg), 1)
    rows_of = [jnp.where(tok < n_tokens, pos_t[2 * TOP_K + k:2 * TOP_K + k + 1, :], -1) for k in range(TOP_K)]
    h = jnp.where(g < n_prompt_groups, hp_ref[...], hs_ref[...])
    for c in range(n_buf // DISPATCH_CHUNK):
        r = c * DISPATCH_CHUNK + lax.broadcasted_iota(jnp.int32, (DISPATCH_CHUNK, tg), 0)
        sel = jnp.zeros((DISPATCH_CHUNK, tg), F32)
        for k in range(TOP_K):
            sel = jnp.where(r == rows_of[k], 1.0, sel)
        buf[slot, c * DISPATCH_CHUNK:(c + 1) * DISPATCH_CHUNK, :] = _dot(sel.astype(BF16), h).astype(BF16)

    run_copies(g, slot, wait=False)

    @pl.when(g == 0)
    def _():
        tail_copies(wait=True)

    @pl.when(jnp.logical_and(g == n - 1, g >= 1))
    def _():
        run_copies(g - 1, 1 - slot, wait=True)

    @pl.when(g == n - 1)
    def _():
        run_copies(g, slot, wait=True)


def _dispatch(h_prompt, h_sample, pos, group_tables, tail_tables, n_slots, tg):
    tp, ts = h_prompt.shape[0], h_sample.shape[0]
    assert tp % tg == 0 and ts <= tg
    n_prompt_groups = tp // tg
    h_sample = jnp.pad(h_sample, ((0, tg - ts), (0, 0)))
    n_buf = tg * TOP_K + N_EXPERTS * RUN_ROWS
    grid_spec = pltpu.PrefetchScalarGridSpec(
        num_scalar_prefetch=6,
        grid=(n_prompt_groups + 1,),
        in_specs=[pl.BlockSpec((tg, D_MODEL), lambda g, *_: (jnp.minimum(g, n_prompt_groups - 1), 0)),
                  pl.BlockSpec((tg, D_MODEL), lambda g, *_: (0, 0)),
                  pl.BlockSpec((tg, ROUTER_LANES), lambda g, *_: (g, 0))],
        out_specs=pl.BlockSpec(memory_space=pl.ANY),
        scratch_shapes=[pltpu.VMEM((2, n_buf, D_MODEL), BF16), pltpu.VMEM((MOE_BLOCK, D_MODEL), BF16),
                        pltpu.SemaphoreType.DMA((2,)), pltpu.SemaphoreType.DMA(())],
    )
    return pl.pallas_call(
        functools.partial(_dispatch_kernel, n_tokens=tp + ts, n_prompt_groups=n_prompt_groups),
        grid_spec=grid_spec,
        out_shape=jax.ShapeDtypeStruct((n_slots, D_MODEL), BF16),
        compiler_params=_params(("arbitrary",)),
        name="dispatch",
    )(*group_tables, *tail_tables, h_prompt, h_sample, pos)


def _block_tables(counts, n_blocks):
    padded = (counts + MOE_BLOCK - 1) // MOE_BLOCK * MOE_BLOCK
    pend = jnp.cumsum(padded)
    bstart = jnp.arange(n_blocks, dtype=jnp.int32) * MOE_BLOCK
    block_exp = jnp.minimum(jnp.sum((pend[None, :] <= bstart[:, None]).astype(jnp.int32), axis=1), N_EXPERTS - 1)
    block_valid = (bstart < pend[-1]).astype(jnp.int32)
    used = (counts + RUN_ROWS - 1) // RUN_ROWS
    tail_tables = ((pend - padded) // RUN_ROWS + used, padded // RUN_ROWS - used, pend[-1:] // MOE_BLOCK)
    return block_exp, block_valid, tail_tables


def _layer(l, xp, xs, kv_prompt, kv_sample, cache_k, cache_v, w):
    bsz, seq, _ = xp.shape
    dbsz, dseq, _ = xs.shape
    tp, ts = bsz * seq, dbsz * dseq

    def dense(x2, kv, n_stream, n_seq, chunk, attn_fn):
        q, k_all, v_all, kb, vb, u, zn, znb = _inproj(x2, w['ln_mix_g'][l], w['w_in'][l], w['sgu_ln_g'][l],
                                                      w['sgu_ln_b'][l], w['gavg'], kv, l, w['depth'])
        r3 = lambda a: a.reshape(n_stream, n_seq, A_WIDTH)
        oa = attn_fn(r3(q), r3(kb), r3(vb)).reshape(n_stream * n_seq, A_WIDTH)
        wsp = w['w_spatial'][l][:, :chunk, :chunk]
        bsp = w['b_spatial'][l][:chunk]
        x1, h2, lg = _mixout(x2, oa, u, znb, wsp, bsp, w['out_norm_a_g'][l], w['out_norm_b_g'][l],
                             w['w_out'][l], w['ln_ffn_g'][l], w['wr_hi'][l], w['wr_lo'][l], w['b_router'][l])
        return x1, h2, lg, (k_all, v_all), zn

    x1p, h2p, lgp, kv_prompt, _ = dense(xp.reshape(tp, D_MODEL), kv_prompt, bsz, seq, MLP_CHUNK, _attn_prompt)
    x1s, h2s, lgs, kv_sample, zs = dense(xs.reshape(ts, D_MODEL), kv_sample, dbsz, dseq, dseq,
                                         lambda q, k, v: _attn_sample(q, k, v, cache_k, cache_v, l))

    t = tp + ts
    slots, gates, counts, runs, group_runs = _route(jnp.concatenate([lgp, lgs], axis=0))
    route_tile = t // runs.shape[0]
    n_groups = group_runs.shape[0]
    tg = route_tile * DISPATCH_TILES
    tables = lambda a: tuple(a[:, j, :N_EXPERTS].reshape(-1) for j in range(3))
    run_tables, group_tables = tables(runs), tables(group_runs)
    max_rows = t * TOP_K + n_groups * N_EXPERTS * (RUN_ROWS - 1)
    n_blocks = -(-max_rows // MOE_BLOCK) + N_EXPERTS + 1
    block_exp, block_valid, tail_tables = _block_tables(counts[0, :N_EXPERTS].astype(jnp.int32), n_blocks)
    x_sorted = _dispatch(h2p, h2s, slots, group_tables, tail_tables, n_blocks * MOE_BLOCK, tg)
    out = _moe(block_exp, block_valid, x_sorted, w['w_gate_up'], w['b_gate_up'], w['w_down'], w['b_down'], l)
    final = l == w['depth'] - 1
    yp = _collect(x1p, out, run_tables, slots, gates, w['final_norm_g'], final, 0, route_tile)
    ys = _collect(x1s, out, run_tables, slots, gates, w['final_norm_g'], final, tp, route_tile)
    return (yp.reshape(bsz, seq, D_MODEL), ys.reshape(dbsz, dseq, D_MODEL), kv_prompt, kv_sample,
            zs.reshape(dbsz, dseq, N_B_GROUPS, B_GROUP_DIM))


def _prepare(ln_mix_g, w_in, sgu_ln_g, sgu_ln_b, w_spatial, b_spatial, out_norm_a_g, out_norm_b_g, w_out,
             ln_ffn_g, w_router, b_router, w_gate_up, b_gate_up, w_down, b_down, final_norm_g):
    depth = w_in.shape[0]
    row = lambda a: a.reshape(depth, 1, -1)
    blk = jnp.arange(MLP_CHUNK, dtype=jnp.int32) // CHUNK
    mask = blk[None, :] <= blk[:, None]
    grp = jnp.arange(B_WIDTH, dtype=jnp.int32) // B_GROUP_DIM
    gavg = jnp.where(grp[:, None] == grp[None, :], 1.0 / B_GROUP_DIM, 0.0).astype(BF16)
    wr = jnp.pad(w_router, ((0, 0), (0, 0), (0, ROUTER_LANES - N_EXPERTS)))
    wr_hi = wr.astype(BF16)
    wr_lo = (wr - wr_hi.astype(F32)).astype(BF16)
    return dict(
        depth=depth,
        ln_mix_g=row(ln_mix_g), w_in=w_in.astype(BF16), sgu_ln_g=row(sgu_ln_g), sgu_ln_b=row(sgu_ln_b),
        gavg=gavg,
        w_spatial=jnp.where(mask[None, None], w_spatial, 0.0).astype(BF16),
        b_spatial=jnp.repeat(jnp.swapaxes(b_spatial, 1, 2), B_GROUP_DIM, axis=2),
        out_norm_a_g=row(out_norm_a_g), out_norm_b_g=row(out_norm_b_g), w_out=w_out.astype(BF16),
        ln_ffn_g=row(ln_ffn_g), wr_hi=wr_hi, wr_lo=wr_lo,
        b_router=row(jnp.pad(b_router, ((0, 0), (0, ROUTER_LANES - N_EXPERTS)))),
        w_gate_up=w_gate_up, b_gate_up=b_gate_up[:, :, None, :],
        w_down=w_down, b_down=b_down[:, :, None, :],
        final_norm_g=final_norm_g.reshape(1, -1),
    )


def kernel(x_prompt, x_sample, cache_k, cache_v, ln_mix_g, w_in, sgu_ln_g, sgu_ln_b, w_spatial, b_spatial,
           out_norm_a_g, out_norm_b_g, w_out, ln_ffn_g, w_router, b_router, w_gate_up, b_gate_up, w_down,
           b_down, final_norm_g):
    w = _prepare(ln_mix_g, w_in, sgu_ln_g, sgu_ln_b, w_spatial, b_spatial, out_norm_a_g, out_norm_b_g, w_out,
                 ln_ffn_g, w_router, b_router, w_gate_up, b_gate_up, w_down, b_down, final_norm_g)
    depth, dbsz, past = cache_k.shape[:3]
    ck = cache_k.reshape(depth, dbsz, past, A_WIDTH)
    cv = cache_v.reshape(depth, dbsz, past, A_WIDTH)
    xp, xs = x_prompt, x_sample
    bsz, seq, _ = xp.shape
    dseq = xs.shape[1]
    kv_prompt = kv_sample = None
    zss = []
    for l in range(depth):
        xp, xs, kv_prompt, kv_sample, zs = _layer(l, xp, xs, kv_prompt, kv_sample, ck, cv, w)
        zss.append(zs)
    heads = lambda a, n_stream, n_seq: a.reshape(depth, n_stream, n_seq, N_A_HEADS, HEAD_DIM)
    return (xp, xs, heads(kv_prompt[0], bsz, seq), heads(kv_prompt[1], bsz, seq),
            heads(kv_sample[0], dbsz, dseq), heads(kv_sample[1], dbsz, dseq), jnp.stack(zss))
```

```python
import functools
import math

import jax
import jax.numpy as jnp
from jax import lax
from jax.experimental import pallas as pl
from jax.experimental.pallas import tpu as pltpu

F32 = jnp.float32
BF16 = jnp.bfloat16

D_MODEL = 1024
HEAD_DIM = 64
N_A_HEADS = 8
A_WIDTH = N_A_HEADS * HEAD_DIM
N_B_GROUPS = 8
B_GROUP_DIM = 64
B_WIDTH = N_B_GROUPS * B_GROUP_DIM
CHUNK = 64
MLP_CHUNK = 128
SB_SCALE = HEAD_DIM ** -0.5
N_EXPERTS = 32
TOP_K = 4
D_FF = D_MODEL
SWIGLU_LIMIT = 7.0
SWIGLU_ALPHA = 1.702
NORM_EPS = 1e-5

LANES = 128
HEADS_PER_TILE = LANES // HEAD_DIM
N_HEAD_TILES = A_WIDTH // LANES
ROUTER_LANES = 128
VMEM_LIMIT = 56 * 1024 * 1024
TOKEN_TILE = 256
DENSE_TILE = 512
ATTN_Q_TILE = 128
ATTN_K_TILE = 256
MOE_BLOCK = 512
WEIGHT_CAST_ROWS = 128
FF_CHUNK = 256
RUN_ROWS = 16
DISPATCH_TILES = 4
DISPATCH_CHUNK = 512
ZERO_WEIGHT_LOG = -110.0


def _params(sem):
    return pltpu.CompilerParams(dimension_semantics=sem, vmem_limit_bytes=VMEM_LIMIT)


def _split_bf16(t):
    hi = t.astype(BF16)
    lo = (t - hi.astype(F32)).astype(BF16)
    return hi, lo


def _dot(a, b):
    return jnp.dot(a, b, preferred_element_type=F32)


def _dot_nt(a, b):
    return lax.dot_general(a, b, (((1,), (1,)), ((), ())), preferred_element_type=F32)


def _rms(x, g):
    return x * lax.rsqrt(jnp.mean(x * x, axis=-1, keepdims=True) + NORM_EPS) * g


def _inproj_kernel(*refs, first):
    x_ref, g_ref, w_ref, lng_ref, lnb_ref, gavg_ref = refs[:6]
    q_ref, k_ref, v_ref, kb_ref, vb_ref, u_ref, zn_ref, znb_ref = refs[-8:]
    hn = _rms(x_ref[...], g_ref[...]).astype(BF16)

    def proj(sec):
        return _dot(hn, w_ref[:, sec * A_WIDTH:(sec + 1) * A_WIDTH])

    def put(all_ref, val):
        if first:
            all_ref[0] = val
            if all_ref.shape[0] > 1:
                all_ref[1:] = jnp.zeros((all_ref.shape[0] - 1,) + val.shape, F32)
        else:
            all_ref[...] = val

    q_ref[...] = (proj(0) * SB_SCALE).astype(BF16)
    k = proj(1)
    put(k_ref, k)
    kb_ref[...] = k.astype(BF16)
    v = proj(2)
    put(v_ref, v)
    vb_ref[...] = v.astype(BF16)
    u_ref[...] = jax.nn.gelu(proj(3)).astype(BF16)
    z = jax.nn.gelu(proj(4))
    gavg = gavg_ref[...]

    def group_mean(t):
        hi, lo = _split_bf16(t)
        return _dot(hi, gavg) + _dot(lo, gavg)

    zc = z - group_mean(z)
    var = group_mean(zc * zc)
    zn = zc * lax.rsqrt(var + NORM_EPS) * lng_ref[...] + lnb_ref[...]
    zn_ref[...] = zn
    znb_ref[...] = zn.astype(BF16)


def _inproj(x, g, w_bf, lng, lnb, gavg, kv_all, layer, depth):
    t = x.shape[0]
    tm = min(DENSE_TILE, t)
    row = lambda w: pl.BlockSpec((tm, w), lambda i: (i, 0))
    full = lambda a: pl.BlockSpec(a.shape, lambda i: (0,) * a.ndim)
    sds = lambda w, dt: jax.ShapeDtypeStruct((t, w), dt)
    first = kv_all is None
    if first:
        extra_specs, extra_args, aliases = [], (), {}
        layer_rows = pl.BlockSpec((depth, tm, A_WIDTH), lambda i: (0, i, 0))
    else:
        extra_specs, extra_args, aliases = [pl.BlockSpec(memory_space=pl.ANY)] * 2, tuple(kv_all), {6: 1, 7: 2}
        layer_rows = pl.BlockSpec((None, tm, A_WIDTH), lambda i: (layer, i, 0))
    all_sds = jax.ShapeDtypeStruct((depth, t, A_WIDTH), F32)
    return pl.pallas_call(
        functools.partial(_inproj_kernel, first=first),
        grid=(t // tm,),
        in_specs=[row(D_MODEL), full(g), full(w_bf), full(lng), full(lnb), full(gavg)] + extra_specs,
        out_specs=[row(A_WIDTH), layer_rows, layer_rows] + [row(A_WIDTH)] * 5,
        out_shape=[sds(A_WIDTH, BF16), all_sds, all_sds, sds(A_WIDTH, BF16),
                   sds(A_WIDTH, BF16), sds(B_WIDTH, BF16), sds(B_WIDTH, F32), sds(B_WIDTH, BF16)],
        input_output_aliases=aliases,
        compiler_params=_params(("parallel",)),
        name="inproj",
    )(x, g, w_bf, lng, lnb, gavg, *extra_args)


def _log_terms(z):
    soft = jnp.log(1.0 + jnp.exp(-jnp.abs(z)))
    return jnp.minimum(z, 0.0) - soft, jnp.minimum(-z, 0.0) - soft


def _tri(n):
    j = lax.broadcasted_iota(jnp.int32, (n, n), 0)
    s = lax.broadcasted_iota(jnp.int32, (n, n), 1)
    return jnp.where(j > s, 1.0, 0.0).astype(BF16)


def _later_sum(log_keep, tri):
    hi, lo = _split_bf16(log_keep)
    return _dot(hi, tri) + _dot(lo, tri)


def _attn_kernel(q_ref, kd_ref, vd_ref, kp_ref, vp_ref, o_ref, *, n_past_fn, tk):
    tq = q_ref.shape[0]
    q = q_ref[...].astype(F32)
    lane = lax.broadcasted_iota(jnp.int32, (tq, LANES), 1)
    qs = [jnp.where((lane // HEAD_DIM) == h, q, 0.0).astype(BF16) for h in range(HEADS_PER_TILE)]

    kd = kd_ref[...].astype(BF16)
    vd = vd_ref[...].astype(BF16)
    t_pos = lax.broadcasted_iota(jnp.int32, (tq, tq), 0)
    s_pos = lax.broadcasted_iota(jnp.int32, (tq, tq), 1)
    visible = s_pos < t_pos
    tri_d = _tri(tq)
    accs, carries = [], []
    for qh in qs:
        log_beta, log_keep = _log_terms(_dot_nt(qh, kd))
        log_keep = jnp.where(visible, log_keep, 0.0)
        a = jnp.where(visible, jnp.exp(log_beta + _later_sum(log_keep, tri_d)), 0.0)
        accs.append(_dot(a.astype(BF16), vd))
        carries.append(jnp.sum(log_keep, axis=-1, keepdims=True))

    n_past = n_past_fn()
    tri_p = _tri(tk)

    def carry_max(carries):
        return functools.reduce(jnp.maximum, [jnp.max(c) for c in carries])

    def cond(state):
        j, _, _, cmax = state
        return jnp.logical_and(j < n_past, cmax > ZERO_WEIGHT_LOG)

    def past_block(j, accs, carries, live=None):
        start = pl.multiple_of(jnp.maximum(n_past - 1 - j, 0) * tk, tk)
        kp = kp_ref[pl.ds(start, tk), :].astype(BF16)
        vp = vp_ref[pl.ds(start, tk), :].astype(BF16)
        new_accs, new_carries = [], []
        for qh, acc, carry in zip(qs, accs, carries):
            log_beta, log_keep = _log_terms(_dot_nt(qh, kp))
            a = jnp.exp(log_beta + _later_sum(log_keep, tri_p) + carry)
            av = _dot(a.astype(BF16), vp)
            new_accs.append(acc + (av if live is None else jnp.where(live, av, 0.0)))
            new_carries.append(carry + jnp.sum(log_keep, axis=-1, keepdims=True))
        return tuple(new_accs), tuple(new_carries)

    def body(state):
        j, accs, carries, _ = state
        accs, carries = past_block(j, accs, carries)
        return j + 1, accs, carries, carry_max(carries)

    accs, carries = past_block(0, accs, carries, live=n_past > 0)
    _, accs, _, _ = lax.while_loop(cond, body, (jnp.int32(1), accs, carries, carry_max(carries)))
    out = accs[0]
    for h in range(1, HEADS_PER_TILE):
        out = jnp.where((lane // HEAD_DIM) == h, accs[h], out)
    o_ref[...] = out


def _attn_prompt(q_bf, k_bf, v_bf):
    b, s, _ = q_bf.shape
    tq = min(ATTN_Q_TILE, s)
    tk = tq
    blk = pl.BlockSpec((None, tq, LANES), lambda bi, hp, i: (bi, i, hp))
    seq = pl.BlockSpec((None, s, LANES), lambda bi, hp, i: (bi, 0, hp))
    kern = functools.partial(_attn_kernel, n_past_fn=lambda: pl.program_id(2) * (tq // tk), tk=tk)
    return pl.pallas_call(
        kern,
        grid=(b, N_HEAD_TILES, s // tq),
        in_specs=[blk, blk, blk, seq, seq],
        out_specs=blk,
        out_shape=jax.ShapeDtypeStruct((b, s, A_WIDTH), F32),
        compiler_params=_params(("parallel", "parallel", "arbitrary")),
        name="attn_prompt",
    )(q_bf, k_bf, v_bf, k_bf, v_bf)


def _attn_sample(q_bf, k_bf, v_bf, cache_k, cache_v, layer):
    b, n, _ = q_bf.shape
    p = cache_k.shape[2]
    tk = min(ATTN_K_TILE, p)
    blk = pl.BlockSpec((None, n, LANES), lambda bi, hp: (bi, 0, hp))
    past = pl.BlockSpec((None, None, p, LANES), lambda bi, hp: (layer, bi, 0, hp))
    kern = functools.partial(_attn_kernel, n_past_fn=lambda: p // tk, tk=tk)
    return pl.pallas_call(
        kern,
        grid=(b, N_HEAD_TILES),
        in_specs=[blk, blk, blk, past, past],
        out_specs=blk,
        out_shape=jax.ShapeDtypeStruct((b, n, A_WIDTH), F32),
        compiler_params=_params(("parallel", "parallel")),
        name="attn_sample",
    )(q_bf, k_bf, v_bf, cache_k, cache_v)


def _mixout_kernel(x_ref, oa_ref, u_ref, znb_ref, wsp_ref, bsp_ref, ga_ref, gb_ref, wout_ref,
                   lnf_ref, wrh_ref, wrl_ref, br_ref, x1_ref, h2_ref, lg_ref, mix_ref):
    tm = x_ref.shape[0]
    chunk = wsp_ref.shape[1]
    group = lax.broadcasted_iota(jnp.int32, (chunk, B_WIDTH), 1) // B_GROUP_DIM
    for c in range(tm // chunk):
        zc = znb_ref[c * chunk:(c + 1) * chunk, :]
        m = jnp.zeros((chunk, B_WIDTH), F32)
        for g in range(N_B_GROUPS):
            m = jnp.where(group == g, _dot(wsp_ref[g], zc), m)
        mix_ref[c * chunk:(c + 1) * chunk, :] = m + bsp_ref[...]
    ob = u_ref[...].astype(F32) * mix_ref[...]
    ya = _rms(oa_ref[...], ga_ref[...]).astype(BF16)
    yb = _rms(ob, gb_ref[...]).astype(BF16)
    x1 = x_ref[...] + _dot(ya, wout_ref[:A_WIDTH, :]) + _dot(yb, wout_ref[A_WIDTH:, :])
    x1_ref[...] = x1
    h2 = _rms(x1, lnf_ref[...])
    hi, lo = _split_bf16(h2)
    h2_ref[...] = hi
    wrh = wrh_ref[...]
    lg_ref[...] = _dot(hi, wrh) + _dot(lo, wrh) + _dot(hi, wrl_ref[...]) + br_ref[...]


def _mixout(x, oa, u, znb, wsp, bsp, ga, gb, wout_bf, lnf, wrh, wrl, br):
    t = x.shape[0]
    tm = min(DENSE_TILE, t)
    row = lambda w: pl.BlockSpec((tm, w), lambda i: (i, 0))
    full = lambda a: pl.BlockSpec(a.shape, lambda i: (0,) * a.ndim)
    return pl.pallas_call(
        _mixout_kernel,
        grid=(t // tm,),
        in_specs=[row(D_MODEL), row(A_WIDTH), row(B_WIDTH), row(B_WIDTH), full(wsp), full(bsp),
                  full(ga), full(gb), full(wout_bf), full(lnf), full(wrh), full(wrl), full(br)],
        out_specs=[row(D_MODEL), row(D_MODEL), row(ROUTER_LANES)],
        out_shape=[jax.ShapeDtypeStruct((t, D_MODEL), F32), jax.ShapeDtypeStruct((t, D_MODEL), BF16),
                   jax.ShapeDtypeStruct((t, ROUTER_LANES), F32)],
        scratch_shapes=[pltpu.VMEM((tm, B_WIDTH), F32)],
        compiler_params=_params(("parallel",)),
        name="mixout",
    )(x, oa, u, znb, wsp, bsp, ga, gb, wout_bf, lnf, wrh, wrl, br)


def _moe_kernel(bexp_ref, bval_ref, xs_ref, wgu_ref, bgu_ref, wd_ref, bd_ref, o_ref, wgu_bf, wd_bf):
    i = pl.program_id(0)
    valid = bval_ref[i] > 0
    new_expert = jnp.logical_or(i == 0, bexp_ref[i] != bexp_ref[jnp.maximum(i - 1, 0)])

    @pl.when(jnp.logical_and(valid, new_expert))
    def _():
        def cast_rows(r, _):
            rows = pl.ds(pl.multiple_of(r * WEIGHT_CAST_ROWS, WEIGHT_CAST_ROWS), WEIGHT_CAST_ROWS)
            for c in range(D_FF // FF_CHUNK):
                lo, hi = c * FF_CHUNK, (c + 1) * FF_CHUNK
                wgu_bf[rows, 2 * lo:2 * lo + FF_CHUNK] = wgu_ref[rows, lo:hi].astype(BF16)
                wgu_bf[rows, 2 * lo + FF_CHUNK:2 * hi] = wgu_ref[rows, D_FF + lo:D_FF + hi].astype(BF16)
            wd_bf[rows, :] = wd_ref[rows, :].astype(BF16)
            return 0
        lax.fori_loop(0, D_MODEL // WEIGHT_CAST_ROWS, cast_rows, 0)

    @pl.when(valid)
    def _():
        xs = xs_ref[...]
        acc = None
        for c in range(D_FF // FF_CHUNK):
            lo, hi = c * FF_CHUNK, (c + 1) * FF_CHUNK
            gate_lin = _dot(xs, wgu_bf[:, 2 * lo:2 * hi])
            gate = gate_lin[:, :FF_CHUNK] + bgu_ref[:, lo:hi]
            lin = gate_lin[:, FF_CHUNK:] + bgu_ref[:, D_FF + lo:D_FF + hi]
            gate = jnp.minimum(gate, SWIGLU_LIMIT)
            lin = jnp.clip(lin, -SWIGLU_LIMIT, SWIGLU_LIMIT)
            act = gate * jax.nn.sigmoid(SWIGLU_ALPHA * gate) * (lin + 1.0)
            part = _dot(act.astype(BF16), wd_bf[lo:hi, :])
            acc = part if acc is None else acc + part
        o_ref[...] = (acc + bd_ref[...]).astype(o_ref.dtype)

    @pl.when(bval_ref[i] == 0)
    def _():
        o_ref[...] = jnp.zeros_like(o_ref)


def _moe(block_exp, block_valid, xs, wgu, bgu, wd, bd, layer):
    n_slots = xs.shape[0]
    bm = MOE_BLOCK
    grid_spec = pltpu.PrefetchScalarGridSpec(
        num_scalar_prefetch=2,
        grid=(n_slots // bm,),
        in_specs=[
            pl.BlockSpec((bm, D_MODEL), lambda i, be, bv: (i, 0)),
            pl.BlockSpec((None, None, D_MODEL, 2 * D_FF), lambda i, be, bv: (layer, be[i], 0, 0)),
            pl.BlockSpec((None, None, 1, 2 * D_FF), lambda i, be, bv: (layer, be[i], 0, 0)),
            pl.BlockSpec((None, None, D_FF, D_MODEL), lambda i, be, bv: (layer, be[i], 0, 0)),
            pl.BlockSpec((None, None, 1, D_MODEL), lambda i, be, bv: (layer, be[i], 0, 0)),
        ],
        out_specs=pl.BlockSpec((bm, D_MODEL), lambda i, be, bv: (i, 0)),
        scratch_shapes=[pltpu.VMEM((D_MODEL, 2 * D_FF), BF16), pltpu.VMEM((D_FF, D_MODEL), BF16)],
    )
    return pl.pallas_call(
        _moe_kernel,
        grid_spec=grid_spec,
        out_shape=jax.ShapeDtypeStruct((n_slots, D_MODEL), BF16),
        compiler_params=_params(("arbitrary",)),
        name="moe",
    )(block_exp, block_valid, xs, wgu, bgu, wd, bd)


def _collect_kernel(src_ref, cnt_ref, dst_ref, x_ref, pos_ref, gates_ref, g_ref, rows_hbm, y_ref, buf, sem,
                    *, final, tile_off):
    i = pl.program_id(0)
    n = pl.num_programs(0)

    def run_copies(tile, slot, wait):
        def per_expert(e, _):
            idx = (tile + tile_off) * N_EXPERTS + e
            src, dst = src_ref[idx], dst_ref[idx]

            def per_piece(c, _):
                cp = pltpu.make_async_copy(
                    rows_hbm.at[pl.ds(pl.multiple_of((src + c) * RUN_ROWS, RUN_ROWS), RUN_ROWS), :],
                    buf.at[slot, pl.ds(pl.multiple_of((dst + c) * RUN_ROWS, RUN_ROWS), RUN_ROWS), :],
                    sem.at[slot])
                if wait:
                    cp.wait()
                else:
                    cp.start()
                return 0

            lax.fori_loop(0, cnt_ref[idx], per_piece, 0)
            return 0

        lax.fori_loop(0, N_EXPERTS, per_expert, 0)

    @pl.when(i == 0)
    def _():
        buf[...] = jnp.zeros_like(buf)
        run_copies(0, 0, wait=False)

    @pl.when(i + 1 < n)
    def _():
        run_copies(i + 1, (i + 1) % 2, wait=False)

    slot = i % 2
    run_copies(i, slot, wait=True)

    tm = x_ref.shape[0]
    n_buf = buf.shape[1]
    pos = pos_ref[...]
    gates = gates_ref[...]
    r = lax.broadcasted_iota(jnp.int32, (tm, n_buf), 1)
    sel = jnp.zeros((tm, n_buf), F32)
    for k in range(TOP_K):
        sel = jnp.where(r == pos[:, TOP_K + k:TOP_K + k + 1], gates[:, k:k + 1], sel)
    hi, lo = _split_bf16(sel)
    rows = buf[slot]
    y = x_ref[...] + _dot(hi, rows) + _dot(lo, rows)
    y_ref[...] = _rms(y, g_ref[...]) if final else y


def _collect(x1, rows, runs, pos, gates, g, final, row_offset, tm):
    t = x1.shape[0]
    off = row_offset // tm
    n_buf = tm * TOP_K + 2 * N_EXPERTS * RUN_ROWS
    grid_spec = pltpu.PrefetchScalarGridSpec(
        num_scalar_prefetch=3,
        grid=(t // tm,),
        in_specs=[pl.BlockSpec((tm, D_MODEL), lambda i, *_: (i, 0)),
                  pl.BlockSpec((tm, ROUTER_LANES), lambda i, *_: (i + off, 0)),
                  pl.BlockSpec((tm, ROUTER_LANES), lambda i, *_: (i + off, 0)),
                  pl.BlockSpec(g.shape, lambda i, *_: (0, 0)),
                  pl.BlockSpec(memory_space=pl.ANY)],
        out_specs=pl.BlockSpec((tm, D_MODEL), lambda i, *_: (i, 0)),
        scratch_shapes=[pltpu.VMEM((2, n_buf, D_MODEL), BF16), pltpu.SemaphoreType.DMA((2,))],
    )
    return pl.pallas_call(
        functools.partial(_collect_kernel, final=final, tile_off=off),
        grid_spec=grid_spec,
        out_shape=jax.ShapeDtypeStruct((t, D_MODEL), F32),
        compiler_params=_params(("arbitrary",)),
        name="collect",
    )(*runs, x1, pos, gates, g, rows)


def _lane_prefix_sum(x):
    lane = lax.broadcasted_iota(jnp.int32, x.shape, 1)
    shift = 1
    while shift < ROUTER_LANES:
        x = x + jnp.where(lane >= shift, pltpu.roll(x, shift, axis=1), 0.0)
        shift *= 2
    return x


def _route_kernel(lg_ref, slots_ref, gates_ref, counts_ref, runs_ref, group_runs_ref,
                  base_ref, pstart_ref, group_start_ref, group_buf_ref, hist_ref):
    phase = pl.program_id(0)
    i = pl.program_id(1)
    group = i // DISPATCH_TILES
    first_in_group = i % DISPATCH_TILES == 0
    pieces_of = lambda rows: jnp.ceil(rows * (1.0 / RUN_ROWS))
    tm = lg_ref.shape[0]
    lane = lax.broadcasted_iota(jnp.int32, (tm, ROUTER_LANES), 1)
    lane_f = lane.astype(F32)
    x = jnp.where(lane < N_EXPERTS, lg_ref[...], -jnp.inf)
    vals, idxs = [], []
    sel = jnp.zeros((tm, ROUTER_LANES), F32)
    for _ in range(TOP_K):
        m = jnp.max(x, axis=-1, keepdims=True)
        idx = jnp.min(jnp.where(x == m, lane_f, float(ROUTER_LANES)), axis=-1, keepdims=True)
        hit = lane_f == idx
        x = jnp.where(hit, -jnp.inf, x)
        sel = jnp.where(hit, 1.0, sel)
        vals.append(m)
        idxs.append(idx)
    tile_counts = jnp.sum(sel, axis=0, keepdims=True)

    @pl.when(jnp.logical_and(phase == 0, i == 0))
    def _():
        base_ref[...] = jnp.zeros_like(base_ref)

    @pl.when(jnp.logical_and(phase == 0, first_in_group))
    def _():
        base_ref[...] = pieces_of(base_ref[...]) * RUN_ROWS
        hist_ref[group] = jnp.zeros_like(base_ref)

    @pl.when(phase == 0)
    def _():
        base_ref[...] = base_ref[...] + tile_counts
        hist_ref[group] = hist_ref[group] + tile_counts

    @pl.when(jnp.logical_and(phase == 1, i == 0))
    def _():
        counts = base_ref[...]
        counts_ref[...] = counts
        padded = jnp.ceil(counts * (1.0 / MOE_BLOCK)) * MOE_BLOCK
        pstart_ref[...] = _lane_prefix_sum(padded) - padded
        base_ref[...] = jnp.zeros_like(base_ref)

    @pl.when(jnp.logical_and(phase == 1, first_in_group))
    def _():
        base = pieces_of(base_ref[...]) * RUN_ROWS
        base_ref[...] = base
        group_start = base + pstart_ref[...]
        group_pieces = pieces_of(hist_ref[group])
        group_buf = (_lane_prefix_sum(group_pieces) - group_pieces) * RUN_ROWS
        group_start_ref[...] = group_start
        group_buf_ref[...] = group_buf
        sub = lax.broadcasted_iota(jnp.int32, base.shape, 0)
        table = jnp.where(sub == 0, group_start * (1.0 / RUN_ROWS),
                          jnp.where(sub == 1, group_pieces, group_buf * (1.0 / RUN_ROWS)))
        group_runs_ref[...] = table.astype(jnp.int32)

    @pl.when(phase == 1)
    def _():
        r = lax.broadcasted_iota(jnp.int32, (tm, tm), 0)
        c = lax.broadcasted_iota(jnp.int32, (tm, tm), 1)
        earlier = jnp.where(c < r, 1.0, 0.0).astype(BF16)
        run_start = base_ref[...] + pstart_ref[...]
        pos = _dot(earlier, sel.astype(BF16)) + run_start[0:1, :]
        aligned = jnp.floor(run_start * (1.0 / RUN_ROWS)) * RUN_ROWS
        pieces = jnp.where(tile_counts > 0.0, jnp.ceil((run_start - aligned + tile_counts) * (1.0 / RUN_ROWS)), 0.0)
        buf_base = (_lane_prefix_sum(pieces) - pieces) * RUN_ROWS
        to_buf = (buf_base - aligned)[0:1, :]
        to_group_buf = (group_buf_ref[...] - group_start_ref[...])[0:1, :]
        weights = [jnp.exp(v - vals[0]) for v in vals]
        total = functools.reduce(lambda a, b: a + b, weights)
        slots = jnp.zeros((tm, ROUTER_LANES), jnp.int32)
        gates = jnp.zeros((tm, ROUTER_LANES), F32)
        for k in range(TOP_K):
            mine = lane_f == idxs[k]
            slot_k = jnp.sum(jnp.where(mine, pos, 0.0), axis=-1, keepdims=True)
            buf_k = jnp.sum(jnp.where(mine, pos + to_buf, 0.0), axis=-1, keepdims=True)
            group_buf_k = jnp.sum(jnp.where(mine, pos + to_group_buf, 0.0), axis=-1, keepdims=True)
            slots = jnp.where(lane == k, slot_k.astype(jnp.int32), slots)
            slots = jnp.where(lane == TOP_K + k, buf_k.astype(jnp.int32), slots)
            slots = jnp.where(lane == 2 * TOP_K + k, group_buf_k.astype(jnp.int32), slots)
            gates = jnp.where(lane == k, weights[k] / total, gates)
        slots_ref[...] = slots
        gates_ref[...] = gates
        sub = lax.broadcasted_iota(jnp.int32, run_start.shape, 0)
        runs = jnp.where(sub == 0, aligned * (1.0 / RUN_ROWS),
                         jnp.where(sub == 1, pieces, buf_base * (1.0 / RUN_ROWS)))
        runs_ref[...] = runs.astype(jnp.int32)
        base_ref[...] = base_ref[...] + tile_counts


def _route(logits):
    t = logits.shape[0]
    tm = math.gcd(TOKEN_TILE, t)
    n_tiles = t // tm
    n_groups = -(-n_tiles // DISPATCH_TILES)
    out_blk = pl.BlockSpec((tm, ROUTER_LANES), lambda p, i: (i * p, 0))
    sub = 8
    vec = pltpu.VMEM((sub, ROUTER_LANES), F32)
    return pl.pallas_call(
        _route_kernel,
        grid=(2, n_tiles),
        in_specs=[pl.BlockSpec((tm, ROUTER_LANES), lambda p, i: (i, 0))],
        out_specs=[out_blk, out_blk, pl.BlockSpec((sub, ROUTER_LANES), lambda p, i: (0, 0)),
                   pl.BlockSpec((None, sub, ROUTER_LANES), lambda p, i: (i * p, 0, 0)),
                   pl.BlockSpec((None, sub, ROUTER_LANES), lambda p, i: (i // DISPATCH_TILES * p, 0, 0))],
        out_shape=[jax.ShapeDtypeStruct((t, ROUTER_LANES), jnp.int32),
                   jax.ShapeDtypeStruct((t, ROUTER_LANES), F32),
                   jax.ShapeDtypeStruct((sub, ROUTER_LANES), F32),
                   jax.ShapeDtypeStruct((n_tiles, sub, ROUTER_LANES), jnp.int32),
                   jax.ShapeDtypeStruct((n_groups, sub, ROUTER_LANES), jnp.int32)],
        scratch_shapes=[vec, vec, vec, vec, pltpu.VMEM((n_groups, sub, ROUTER_LANES), F32)],
        compiler_params=_params(("arbitrary", "arbitrary")),
        name="route",
    )(logits)


def _dispatch_kernel(dst_ref, cnt_ref, src_ref, tail_ref, tail_cnt_ref, valid_blocks_ref, hp_ref, hs_ref, pos_ref, xs_hbm,
                     buf, zeros, sem, tail_sem, *, n_tokens, n_prompt_groups):
    g = pl.program_id(0)
    n = pl.num_programs(0)
    tg = hp_ref.shape[0]
    n_buf = buf.shape[1]
    slot = g % 2

    def tail_copies(wait):
        def go(cp):
            if wait:
                cp.wait()
            else:
                cp.start()
            return 0

        def per_expert(e, _):
            def per_piece(c, _):
                rows = pl.ds(pl.multiple_of((tail_ref[e] + c) * RUN_ROWS, RUN_ROWS), RUN_ROWS)
                return go(pltpu.make_async_copy(zeros.at[pl.ds(0, RUN_ROWS), :], xs_hbm.at[rows, :], tail_sem))

            lax.fori_loop(0, tail_cnt_ref[e], per_piece, 0)
            return 0

        def per_block(b, _):
            rows = pl.ds(pl.multiple_of(b * MOE_BLOCK, MOE_BLOCK), MOE_BLOCK)
            return go(pltpu.make_async_copy(zeros, xs_hbm.at[rows, :], tail_sem))

        lax.fori_loop(0, N_EXPERTS, per_expert, 0)
        lax.fori_loop(valid_blocks_ref[0], xs_hbm.shape[0] // MOE_BLOCK, per_block, 0)

    @pl.when(g == 0)
    def _():
        zeros[...] = jnp.zeros_like(zeros)
        tail_copies(wait=False)

    def run_copies(grp, slot, wait):
        def per_expert(e, _):
            idx = grp * N_EXPERTS + e
            dst, src = dst_ref[idx], src_ref[idx]

            def per_piece(c, _):
                cp = pltpu.make_async_copy(
                    buf.at[slot, pl.ds(pl.multiple_of((src + c) * RUN_ROWS, RUN_ROWS), RUN_ROWS), :],
                    xs_hbm.at[pl.ds(pl.multiple_of((dst + c) * RUN_ROWS, RUN_ROWS), RUN_ROWS), :],
                    sem.at[slot])
                if wait:
                    cp.wait()
                else:
                    cp.start()
                return 0

            lax.fori_loop(0, cnt_ref[idx], per_piece, 0)
            return 0

        lax.fori_loop(0, N_EXPERTS, per_expert, 0)

    @pl.when(g >= 2)
    def _():
        run_copies(g - 2, slot, wait=True)

    pos_t = pos_ref[...].T
    tok = g * tg + lax.broadcasted_iota(jnp.int32, (1, tg), 1)
    rows_of = [jnp.where(tok < n_tokens, pos_t[2 * TOP_K + k:2 * TOP_K + k + 1, :], -1) for k in range(TOP_K)]
    h = jnp.where(g < n_prompt_groups, hp_ref[...], hs_ref[...])
    for c in range(n_buf // DISPATCH_CHUNK):
        r = c * DISPATCH_CHUNK + lax.broadcasted_iota(jnp.int32, (DISPATCH_CHUNK, tg), 0)
        sel = jnp.zeros((DISPATCH_CHUNK, tg), F32)
        for k in range(TOP_K):
            sel = jnp.where(r == rows_of[k], 1.0, sel)
        buf[slot, c * DISPATCH_CHUNK:(c + 1) * DISPATCH_CHUNK, :] = _dot(sel.astype(BF16), h).astype(BF16)

    run_copies(g, slot, wait=False)

    @pl.when(g == 0)
    def _():
        tail_copies(wait=True)

    @pl.when(jnp.logical_and(g == n - 1, g >= 1))
    def _():
        run_copies(g - 1, 1 - slot, wait=True)

    @pl.when(g == n - 1)
    def _():
        run_copies(g, slot, wait=True)


def _dispatch(h_prompt, h_sample, pos, group_tables, tail_tables, n_slots, tg):
    tp, ts = h_prompt.shape[0], h_sample.shape[0]
    assert tp % tg == 0 and ts <= tg
    n_prompt_groups = tp // tg
    h_sample = jnp.pad(h_sample, ((0, tg - ts), (0, 0)))
    n_buf = tg * TOP_K + N_EXPERTS * RUN_ROWS
    grid_spec = pltpu.PrefetchScalarGridSpec(
        num_scalar_prefetch=6,
        grid=(n_prompt_groups + 1,),
        in_specs=[pl.BlockSpec((tg, D_MODEL), lambda g, *_: (jnp.minimum(g, n_prompt_groups - 1), 0)),
                  pl.BlockSpec((tg, D_MODEL), lambda g, *_: (0, 0)),
                  pl.BlockSpec((tg, ROUTER_LANES), lambda g, *_: (g, 0))],
        out_specs=pl.BlockSpec(memory_space=pl.ANY),
        scratch_shapes=[pltpu.VMEM((2, n_buf, D_MODEL), BF16), pltpu.VMEM((MOE_BLOCK, D_MODEL), BF16),
                        pltpu.SemaphoreType.DMA((2,)), pltpu.SemaphoreType.DMA(())],
    )
    return pl.pallas_call(
        functools.partial(_dispatch_kernel, n_tokens=tp + ts, n_prompt_groups=n_prompt_groups),
        grid_spec=grid_spec,
        out_shape=jax.ShapeDtypeStruct((n_slots, D_MODEL), BF16),
        compiler_params=_params(("arbitrary",)),
        name="dispatch",
    )(*group_tables, *tail_tables, h_prompt, h_sample, pos)


def _block_tables(counts, n_blocks):
    padded = (counts + MOE_BLOCK - 1) // MOE_BLOCK * MOE_BLOCK
    pend = jnp.cumsum(padded)
    bstart = jnp.arange(n_blocks, dtype=jnp.int32) * MOE_BLOCK
    block_exp = jnp.minimum(jnp.sum((pend[None, :] <= bstart[:, None]).astype(jnp.int32), axis=1), N_EXPERTS - 1)
    block_valid = (bstart < pend[-1]).astype(jnp.int32)
    used = (counts + RUN_ROWS - 1) // RUN_ROWS
    tail_tables = ((pend - padded) // RUN_ROWS + used, padded // RUN_ROWS - used, pend[-1:] // MOE_BLOCK)
    return block_exp, block_valid, tail_tables


def _layer(l, xp, xs, kv_prompt, kv_sample, cache_k, cache_v, w):
    bsz, seq, _ = xp.shape
    dbsz, dseq, _ = xs.shape
    tp, ts = bsz * seq, dbsz * dseq

    def dense(x2, kv, n_stream, n_seq, chunk, attn_fn):
        q, k_all, v_all, kb, vb, u, zn, znb = _inproj(x2, w['ln_mix_g'][l], w['w_in'][l], w['sgu_ln_g'][l],
                                                      w['sgu_ln_b'][l], w['gavg'], kv, l, w['depth'])
        r3 = lambda a: a.reshape(n_stream, n_seq, A_WIDTH)
        oa = attn_fn(r3(q), r3(kb), r3(vb)).reshape(n_stream * n_seq, A_WIDTH)
        wsp = w['w_spatial'][l][:, :chunk, :chunk]
        bsp = w['b_spatial'][l][:chunk]
        x1, h2, lg = _mixout(x2, oa, u, znb, wsp, bsp, w['out_norm_a_g'][l], w['out_norm_b_g'][l],
                             w['w_out'][l], w['ln_ffn_g'][l], w['wr_hi'][l], w['wr_lo'][l], w['b_router'][l])
        return x1, h2, lg, (k_all, v_all), zn

    x1p, h2p, lgp, kv_prompt, _ = dense(xp.reshape(tp, D_MODEL), kv_prompt, bsz, seq, MLP_CHUNK, _attn_prompt)
    x1s, h2s, lgs, kv_sample, zs = dense(xs.reshape(ts, D_MODEL), kv_sample, dbsz, dseq, dseq,
                                         lambda q, k, v: _attn_sample(q, k, v, cache_k, cache_v, l))

    t = tp + ts
    slots, gates, counts, runs, group_runs = _route(jnp.concatenate([lgp, lgs], axis=0))
    route_tile = t // runs.shape[0]
    n_groups = group_runs.shape[0]
    tg = route_tile * DISPATCH_TILES
    tables = lambda a: tuple(a[:, j, :N_EXPERTS].reshape(-1) for j in range(3))
    run_tables, group_tables = tables(runs), tables(group_runs)
    max_rows = t * TOP_K + n_groups * N_EXPERTS * (RUN_ROWS - 1)
    n_blocks = -(-max_rows // MOE_BLOCK) + N_EXPERTS + 1
    block_exp, block_valid, tail_tables = _block_tables(counts[0, :N_EXPERTS].astype(jnp.int32), n_blocks)
    x_sorted = _dispatch(h2p, h2s, slots, group_tables, tail_tables, n_blocks * MOE_BLOCK, tg)
    out = _moe(block_exp, block_valid, x_sorted, w['w_gate_up'], w['b_gate_up'], w['w_down'], w['b_down'], l)
    final = l == w['depth'] - 1
    yp = _collect(x1p, out, run_tables, slots, gates, w['final_norm_g'], final, 0, route_tile)
    ys = _collect(x1s, out, run_tables, slots, gates, w['final_norm_g'], final, tp, route_tile)
    return (yp.reshape(bsz, seq, D_MODEL), ys.reshape(dbsz, dseq, D_MODEL), kv_prompt, kv_sample,
            zs.reshape(dbsz, dseq, N_B_GROUPS, B_GROUP_DIM))


def _prepare(ln_mix_g, w_in, sgu_ln_g, sgu_ln_b, w_spatial, b_spatial, out_norm_a_g, out_norm_b_g, w_out,
             ln_ffn_g, w_router, b_router, w_gate_up, b_gate_up, w_down, b_down, final_norm_g):
    depth = w_in.shape[0]
    row = lambda a: a.reshape(depth, 1, -1)
    blk = jnp.arange(MLP_CHUNK, dtype=jnp.int32) // CHUNK
    mask = blk[None, :] <= blk[:, None]
    grp = jnp.arange(B_WIDTH, dtype=jnp.int32) // B_GROUP_DIM
    gavg = jnp.where(grp[:, None] == grp[None, :], 1.0 / B_GROUP_DIM, 0.0).astype(BF16)
    wr = jnp.pad(w_router, ((0, 0), (0, 0), (0, ROUTER_LANES - N_EXPERTS)))
    wr_hi = wr.astype(BF16)
    wr_lo = (wr - wr_hi.astype(F32)).astype(BF16)
    return dict(
        depth=depth,
        ln_mix_g=row(ln_mix_g), w_in=w_in.astype(BF16), sgu_ln_g=row(sgu_ln_g), sgu_ln_b=row(sgu_ln_b),
        gavg=gavg,
        w_spatial=jnp.where(mask[None, None], w_spatial, 0.0).astype(BF16),
        b_spatial=jnp.repeat(jnp.swapaxes(b_spatial, 1, 2), B_GROUP_DIM, axis=2),
        out_norm_a_g=row(out_norm_a_g), out_norm_b_g=row(out_norm_b_g), w_out=w_out.astype(BF16),
        ln_ffn_g=row(ln_ffn_g), wr_hi=wr_hi, wr_lo=wr_lo,
        b_router=row(jnp.pad(b_router, ((0, 0), (0, ROUTER_LANES - N_EXPERTS)))),
        w_gate_up=w_gate_up, b_gate_up=b_gate_up[:, :, None, :],
        w_down=w_down, b_down=b_down[:, :, None, :],
        final_norm_g=final_norm_g.reshape(1, -1),
    )


def kernel(x_prompt, x_sample, cache_k, cache_v, ln_mix_g, w_in, sgu_ln_g, sgu_ln_b, w_spatial, b_spatial,
           out_norm_a_g, out_norm_b_g, w_out, ln_ffn_g, w_router, b_router, w_gate_up, b_gate_up, w_down,
           b_down, final_norm_g):
    w = _prepare(ln_mix_g, w_in, sgu_ln_g, sgu_ln_b, w_spatial, b_spatial, out_norm_a_g, out_norm_b_g, w_out,
                 ln_ffn_g, w_router, b_router, w_gate_up, b_gate_up, w_down, b_down, final_norm_g)
    depth, dbsz, past = cache_k.shape[:3]
    ck = cache_k.reshape(depth, dbsz, past, A_WIDTH)
    cv = cache_v.reshape(depth, dbsz, past, A_WIDTH)
    xp, xs = x_prompt, x_sample
    bsz, seq, _ = xp.shape
    dseq = xs.shape[1]
    kv_prompt = kv_sample = None
    zss = []
    for l in range(depth):
        xp, xs, kv_prompt, kv_sample, zs = _layer(l, xp, xs, kv_prompt, kv_sample, ck, cv, w)
        zss.append(zs)
    heads = lambda a, n_stream, n_seq: a.reshape(depth, n_stream, n_seq, N_A_HEADS, HEAD_DIM)
    return (xp, xs, heads(kv_prompt[0], bsz, seq), heads(kv_prompt[1], bsz, seq),
            heads(kv_sample[0], dbsz, dseq), heads(kv_sample[1], dbsz, dseq), jnp.stack(zss))
```
